```python
import jax, jax.numpy as jnp
from jax import lax
import numpy as np

D_MODEL = 1024
BATCH = 8
SEQ = 4096
DEPTH = 2

N_A = DEPTH // 2
N_B = DEPTH - N_A
N_DENSE = (DEPTH + 1) // 2
N_MOE = DEPTH // 2

GLA_HEADS = 4
GLA_DK = D_MODEL // 2 // GLA_HEADS
GLA_DV = D_MODEL // GLA_HEADS
GLA_QK = GLA_HEADS * GLA_DK
GLA_V = GLA_HEADS * GLA_DV
GLA_GATE_RANK = 16
GLA_GATE_NORM = 16.0
GLA_CHUNK = 64
GLA_IN = 2 * GLA_QK + GLA_V + GLA_GATE_RANK + GLA_V

SB_HEADS = 8
SB_HEAD_DIM = D_MODEL // SB_HEADS
SB_BLOCK = 128

FFN_DIM = 2816
N_EXPERTS = 8
TOP_K = 2
EXPERT_DIM = 3584

RMS_EPS = 1e-6

kernel_name = "yoco_gla_stickbreaking_moe"


def rms_norm(x, g):
    xf = x.astype(jnp.float32)
    y = xf * lax.rsqrt(jnp.mean(xf * xf, axis=-1, keepdims=True) + RMS_EPS)
    return (y * g.astype(jnp.float32)).astype(x.dtype)


def swiglu(h, w_gate, w_up, w_down):
    return (jax.nn.silu(h @ w_gate) * (h @ w_up)) @ w_down


def gla_mixer(h, w_in, w_gate_up, b_gate, g_head, w_out):
    B, S, _ = h.shape
    H, dk, dv, C = GLA_HEADS, GLA_DK, GLA_DV, GLA_CHUNK
    nc = S // C
    proj = h @ w_in
    q, k, v, g_lr, r = jnp.split(
        proj, [GLA_QK, 2 * GLA_QK, 2 * GLA_QK + GLA_V, 2 * GLA_QK + GLA_V + GLA_GATE_RANK], axis=-1)
    log_a = jax.nn.log_sigmoid((g_lr @ w_gate_up + b_gate).astype(jnp.float32)) / GLA_GATE_NORM

    def to_chunks(t, d):
        return t.reshape(B, nc, C, H, d).transpose(1, 0, 3, 2, 4).astype(jnp.float32)

    qc = to_chunks(q, dk) * (dk ** -0.5)
    kc = to_chunks(k, dk)
    vc = to_chunks(v, dv)
    gc = to_chunks(log_a, dk)
    b = jnp.cumsum(gc, axis=3)
    b_end = b[:, :, :, -1:, :]
    q_dec = qc * jnp.exp(b)
    k_inv = kc * jnp.exp(-b)
    k_end = kc * jnp.exp(b_end - b)
    decay_end = jnp.exp(b_end[:, :, :, 0, :])

    tri = jnp.tril(jnp.ones((C, C), dtype=bool))
    attn = jnp.where(tri, jnp.einsum('nbhid,nbhjd->nbhij', q_dec, k_inv), 0.0)
    o_intra = jnp.einsum('nbhij,nbhjv->nbhiv', attn, vc)

    def step(state, inp):
        q_n, k_n, v_n, dec_n = inp
        o_n = jnp.einsum('bhid,bhdv->bhiv', q_n, state)
        state = dec_n[..., None] * state + jnp.einsum('bhid,bhiv->bhdv', k_n, v_n)
        return state, o_n

    state0 = jnp.zeros((B, H, dk, dv), dtype=jnp.float32)
    _, o_inter = lax.scan(step, state0, (q_dec, k_end, vc, decay_end))
    o = (o_inter + o_intra).transpose(1, 0, 3, 2, 4).reshape(B, S, H, dv)
    o = rms_norm(o, g_head).reshape(B, S, GLA_V)
    o = (o * jax.nn.silu(r.astype(jnp.float32))).astype(h.dtype)
    return o @ w_out


def stick_breaking_mixer(h, k, v, w_q, w_out):
    B, S, _ = h.shape
    H, hd = SB_HEADS, SB_HEAD_DIM
    q = (h @ w_q).reshape(B, S, H, hd).transpose(0, 2, 1, 3)
    scale = hd ** -0.5
    outs = []
    for blk in range(S // SB_BLOCK):
        t0 = blk * SB_BLOCK
        t1 = t0 + SB_BLOCK
        qb = q[:, :, t0:t1]
        kb = k[:, :, :t1]
        vb = v[:, :, :t1]
        z = jnp.einsum('bhtd,bhsd->bhts', qb, kb).astype(jnp.float32) * scale
        t_idx = t0 + jnp.arange(SB_BLOCK)[:, None]
        s_idx = jnp.arange(t1)[None, :]
        causal = s_idx < t_idx
        log_beta = jax.nn.log_sigmoid(z)
        log_keep = jnp.where(causal, jax.nn.log_sigmoid(-z), 0.0)
        log_stay = lax.cumsum(log_keep, axis=3, reverse=True) - log_keep
        w = jnp.where(causal, jnp.exp(log_beta + log_stay), 0.0)
        outs.append(jnp.einsum('bhts,bhsd->bhtd', w.astype(vb.dtype), vb))
    o = jnp.concatenate(outs, axis=2).transpose(0, 2, 1, 3).reshape(B, S, H * hd)
    return o @ w_out


def moe_swiglu(h, w_router, w_gate, w_up, w_down):
    logits = (h @ w_router).astype(jnp.float32)
    top_val, top_idx = lax.top_k(logits, TOP_K)
    top_w = jax.nn.softmax(top_val, axis=-1)
    gates = jnp.sum(jax.nn.one_hot(top_idx, N_EXPERTS, dtype=jnp.float32) * top_w[..., None], axis=-2)
    out = jnp.zeros_like(h)
    for e in range(N_EXPERTS):
        y = swiglu(h, w_gate[e], w_up[e], w_down[e])
        out = out + gates[..., e:e + 1].astype(h.dtype) * y
    return out


def setup_inputs(seed: int = 0) -> dict:
    key = jax.random.key(seed)
    ks = jax.random.split(key, 20)
    D = D_MODEL

    def w(k, shape, fan_in):
        return jax.random.normal(k, shape, dtype=jnp.float32) * (fan_in ** -0.5)

    def gain(k, shape):
        return 1.0 + 0.02 * jax.random.normal(k, shape, dtype=jnp.float32)

    return {
        "x": jax.random.normal(ks[0], (BATCH, SEQ, D), dtype=jnp.float32),
        "attn_norm": gain(ks[1], (DEPTH, D)),
        "ffn_norm": gain(ks[2], (DEPTH, D)),
        "kv_norm": gain(ks[3], (D,)),
        "final_norm": gain(ks[4], (D,)),
        "gla_w_in": w(ks[5], (N_A, D, GLA_IN), D),
        "gla_w_gate_up": w(ks[6], (N_A, GLA_GATE_RANK, GLA_QK), GLA_GATE_RANK),
        "gla_b_gate": 0.1 * jax.random.normal(ks[7], (N_A, GLA_QK), dtype=jnp.float32),
        "gla_head_norm": gain(ks[8], (N_A, GLA_DV)),
        "gla_w_out": w(ks[9], (N_A, GLA_V, D), GLA_V),
        "sb_w_kv": w(ks[10], (D, 2 * SB_HEADS * SB_HEAD_DIM), D),
        "sb_w_q": w(ks[11], (N_B, D, SB_HEADS * SB_HEAD_DIM), D),
        "sb_w_out": w(ks[12], (N_B, SB_HEADS * SB_HEAD_DIM, D), SB_HEADS * SB_HEAD_DIM),
        "ffn_w_gate": w(ks[13], (N_DENSE, D, FFN_DIM), D),
        "ffn_w_up": w(ks[14], (N_DENSE, D, FFN_DIM), D),
        "ffn_w_down": w(ks[15], (N_DENSE, FFN_DIM, D), FFN_DIM),
        "moe_w_router": w(ks[16], (N_MOE, D, N_EXPERTS), D),
        "moe_w_gate": w(ks[17], (N_MOE, N_EXPERTS, D, EXPERT_DIM), D),
        "moe_w_up": w(ks[18], (N_MOE, N_EXPERTS, D, EXPERT_DIM), D),
        "moe_w_down": w(ks[19], (N_MOE, N_EXPERTS, EXPERT_DIM, D), EXPERT_DIM),
    }


def reference(x, attn_norm, ffn_norm, kv_norm, final_norm, gla_w_in, gla_w_gate_up, gla_b_gate,
              gla_head_norm, gla_w_out, sb_w_kv, sb_w_q, sb_w_out, ffn_w_gate, ffn_w_up, ffn_w_down,
              moe_w_router, moe_w_gate, moe_w_up, moe_w_down):
    B, S, D = x.shape
    h = x
    k_shared = None
    v_shared = None
    for layer in range(DEPTH):
        if layer < N_A:
            h = h + gla_mixer(rms_norm(h, attn_norm[layer]), gla_w_in[layer], gla_w_gate_up[layer],
                              gla_b_gate[layer], gla_head_norm[layer], gla_w_out[layer])
        else:
            if layer == N_A:
                kv = rms_norm(h, kv_norm) @ sb_w_kv
                k_s, v_s = jnp.split(kv, 2, axis=-1)
                k_shared = k_s.reshape(B, S, SB_HEADS, SB_HEAD_DIM).transpose(0, 2, 1, 3)
                v_shared = v_s.reshape(B, S, SB_HEADS, SB_HEAD_DIM).transpose(0, 2, 1, 3)
            ib = layer - N_A
            h = h + stick_breaking_mixer(rms_norm(h, attn_norm[layer]), k_shared, v_shared,
                                         sb_w_q[ib], sb_w_out[ib])
        hn = rms_norm(h, ffn_norm[layer])
        if layer % 2 == 0:
            i = layer // 2
            h = h + swiglu(hn, ffn_w_gate[i], ffn_w_up[i], ffn_w_down[i])
        else:
            i = layer // 2
            h = h + moe_swiglu(hn, moe_w_router[i], moe_w_gate[i], moe_w_up[i], moe_w_down[i])
    return rms_norm(h, final_norm)
```

```python
import functools

import jax
import jax.numpy as jnp
from jax import lax
from jax.experimental import pallas as pl
from jax.experimental.pallas import tpu as pltpu

F32 = jnp.float32
BF16 = jnp.bfloat16

RMS_EPS = 1e-6
GLA_HEADS = 4
GLA_GATE_NORM = 16.0
GLA_CHUNK = 64
SB_HEADS = 8
N_EXPERTS = 8

LANES = 128
V7X_VMEM_LIMIT_CAP = 60 * 1024 * 1024

TOKEN_TILE = 512
GLA_TILE = 256
SB_TILE = 256
MOE_TILE = 512
MOE_F_TILE = 512
ROW_TILE = 256


def _vmem_limit(estimate_bytes):
    return int(min(max(estimate_bytes * 5 // 4 + (4 << 20), 16 << 20), V7X_VMEM_LIMIT_CAP))


def _params(semantics, estimate_bytes):
    return pltpu.CompilerParams(dimension_semantics=semantics,
                                vmem_limit_bytes=_vmem_limit(estimate_bytes))


def _rms_scale(x):
    return lax.rsqrt(jnp.mean(x * x, axis=-1, keepdims=True) + RMS_EPS)


def _log_sigmoid(x):
    return jnp.minimum(x, 0.0) - jnp.log(1.0 + jnp.exp(-jnp.abs(x)))


def _silu(x):
    return x * (1.0 / (1.0 + jnp.exp(-x)))


def _dot(a, b):
    return jnp.dot(a, b, preferred_element_type=F32)


def _dot_nt(a, b):
    return lax.dot_general(a, b, (((1,), (1,)), ((), ())), preferred_element_type=F32)


def _dot_tn(a, b):
    return lax.dot_general(a, b, (((0,), (0,)), ((), ())), preferred_element_type=F32)


def _split_bf16(x):
    hi = x.astype(BF16)
    lo = (x - hi.astype(F32)).astype(BF16)
    return hi, lo


def _resident(shape):
    return pl.BlockSpec(shape, lambda *_: (0,) * len(shape))


def _inproj_kernel(x_ref, g_ref, w_ref, qk_ref, v_ref, r_ref, glr_ref, *, d_qk, d_v):
    x = x_ref[...]
    xn = (x * _rms_scale(x) * g_ref[...]).astype(BF16)
    c0, c1, c2 = d_qk, d_qk + d_v, d_qk + 2 * d_v
    qk_ref[...] = _dot(xn, w_ref[:, 0:c0])
    v_ref[...] = _dot(xn, w_ref[:, c0:c1]).astype(BF16)
    r_ref[...] = _dot(xn, w_ref[:, c1:c2])
    glr_ref[...] = _dot(xn, w_ref[:, c2:c2 + LANES])


def _inproj(x, gain, w, d_qk, d_v):
    t, d = x.shape
    n = w.shape[1]
    tm = TOKEN_TILE
    est = 2 * (tm * d * 4 + d * n * 2 + tm * (d_qk * 4 + d_v * 2 + d_v * 4 + LANES * 4)) + 3 * tm * d * 4
    row = lambda i: (i, 0)
    return pl.pallas_call(
        functools.partial(_inproj_kernel, d_qk=d_qk, d_v=d_v),
        out_shape=(jax.ShapeDtypeStruct((t, d_qk), F32), jax.ShapeDtypeStruct((t, d_v), BF16),
                   jax.ShapeDtypeStruct((t, d_v), F32), jax.ShapeDtypeStruct((t, LANES), F32)),
        grid=(t // tm,),
        in_specs=[pl.BlockSpec((tm, d), row), _resident((1, d)), _resident((d, n))],
        out_specs=(pl.BlockSpec((tm, d_qk), row), pl.BlockSpec((tm, d_v), row),
                   pl.BlockSpec((tm, d_v), row), pl.BlockSpec((tm, LANES), row)),
        compiler_params=_params(("parallel",), est),
        name="gla_inproj",
    )(x, gain, w)


def _gla_kernel(qk_ref, v_ref, r_ref, glr_ref, wgu_ref, bg_ref, gh_ref, cum_ref, og_ref, st_ref,
                *, heads, dk, dv, chunk):
    tc = qk_ref.shape[0]
    hk = heads * dk

    @pl.when(pl.program_id(1) == 0)
    def _():
        st_ref[...] = jnp.zeros_like(st_ref)

    pre = _dot(glr_ref[...].astype(BF16), wgu_ref[...]) + bg_ref[...]
    log_a = _log_sigmoid(pre) * (1.0 / GLA_GATE_NORM)
    hi, lo = _split_bf16(log_a)
    cums = _dot(cum_ref[...], jnp.concatenate([hi, lo], axis=0))
    b = cums[:tc]
    b_end = cums[tc:]
    qk = qk_ref[...]
    q_dec = (qk[:, :hk] * (dk ** -0.5) * jnp.exp(b)).astype(BF16)
    k = qk[:, hk:]
    k_inv = (k * jnp.exp(-b)).astype(BF16)
    k_end = (k * jnp.exp(b_end - b)).astype(BF16)
    decay_end = jnp.exp(b_end)

    row = lax.broadcasted_iota(jnp.int32, (chunk, chunk), 0)
    col = lax.broadcasted_iota(jnp.int32, (chunk, chunk), 1)
    tri = col <= row
    gh = gh_ref[...]
    for c in range(tc // chunk):
        rows = slice(c * chunk, (c + 1) * chunk)
        for h in range(heads):
            kc = slice(h * dk, (h + 1) * dk)
            vc = slice(h * dv, (h + 1) * dv)
            qd = q_dec[rows, kc]
            vv = v_ref[rows, vc]
            st = st_ref[h]
            attn = jnp.where(tri, _dot_nt(qd, k_inv[rows, kc]), 0.0).astype(BF16)
            o = _dot_nt(qd, st.astype(BF16)) + _dot(attn, vv)
            st_ref[h] = decay_end[c * chunk:c * chunk + 1, kc] * st + _dot_tn(vv, k_end[rows, kc])
            o = o * _rms_scale(o) * gh
            og_ref[rows, vc] = (o * _silu(r_ref[rows, vc])).astype(BF16)


def _gla_cum_matrix(tc, chunk):
    i = jnp.arange(tc)
    same = (i[:, None] // chunk) == (i[None, :] // chunk)
    lower = jnp.logical_and(same, i[None, :] <= i[:, None])
    m = jnp.concatenate([lower, same], axis=0).astype(BF16)
    return jnp.concatenate([m, m], axis=1)


def _gla(qk, v, r, glr, wgu, bgate, ghead, b_sz, s_len, heads, dk, dv):
    tc = GLA_TILE
    hk, hv = heads * dk, heads * dv
    qk, v, r, glr = (a.reshape(b_sz, s_len, a.shape[-1]) for a in (qk, v, r, glr))
    cum = _gla_cum_matrix(tc, GLA_CHUNK)
    est = 2 * tc * (2 * hk * 4 + hv * 2 + hv * 4 + LANES * 4 + hv * 2) + heads * dv * dk * 4 + 16 * tc * hk * 4
    blk = lambda n: pl.BlockSpec((None, tc, n), lambda b, s: (b, s, 0))
    out = pl.pallas_call(
        functools.partial(_gla_kernel, heads=heads, dk=dk, dv=dv, chunk=GLA_CHUNK),
        out_shape=jax.ShapeDtypeStruct((b_sz, s_len, hv), BF16),
        grid=(b_sz, s_len // tc),
        in_specs=[blk(2 * hk), blk(hv), blk(hv), blk(LANES), _resident(wgu.shape), _resident(bgate.shape),
                  _resident(ghead.shape), _resident(cum.shape)],
        out_specs=blk(hv),
        scratch_shapes=[pltpu.VMEM((heads, dv, dk), F32)],
        compiler_params=_params(("parallel", "arbitrary"), est),
        name="gla_recurrence",
    )(qk, v, r, glr, wgu, bgate, ghead, cum)
    return out.reshape(b_sz * s_len, hv)


def _outproj_kernel(o_ref, w_ref, res_ref, g_ref, h_ref, hn_ref):
    h = res_ref[...] + _dot(o_ref[...], w_ref[...])
    h_ref[...] = h
    hn_ref[...] = (h * _rms_scale(h) * g_ref[...]).astype(BF16)


def _outproj(o, w, res, gain):
    t, d = res.shape
    k = o.shape[1]
    tm = TOKEN_TILE
    est = 2 * (tm * k * 2 + k * d * 2 + 2 * tm * d * 4 + tm * d * 2) + 2 * tm * d * 4
    row = lambda i: (i, 0)
    return pl.pallas_call(
        _outproj_kernel,
        out_shape=(jax.ShapeDtypeStruct((t, d), F32), jax.ShapeDtypeStruct((t, d), BF16)),
        grid=(t // tm,),
        in_specs=[pl.BlockSpec((tm, k), row), _resident((k, d)), pl.BlockSpec((tm, d), row), _resident((1, d))],
        out_specs=(pl.BlockSpec((tm, d), row), pl.BlockSpec((tm, d), row)),
        compiler_params=_params(("parallel",), est),
        name="outproj_residual_norm",
    )(o, w, res, gain)


def _ffn_kernel(hn_ref, wg_ref, wu_ref, wd_ref, res_ref, out_ref, act_ref, *, f_chunk):
    hn = hn_ref[...]
    f_dim = wg_ref.shape[1]
    for c0 in range(0, f_dim, f_chunk):
        cols = slice(c0, min(c0 + f_chunk, f_dim))
        act_ref[:, cols] = (_silu(_dot(hn, wg_ref[:, cols])) * _dot(hn, wu_ref[:, cols])).astype(BF16)
    out_ref[...] = res_ref[...] + _dot(act_ref[...], wd_ref[...])


def _ffn(hn, wg, wu, wd, res):
    t, d = res.shape
    f = wg.shape[1]
    tm = TOKEN_TILE
    est = 2 * (tm * d * 2 + 3 * d * f * 2 + 2 * tm * d * 4) + tm * f * 2 + 6 * tm * 256 * 4
    row = lambda i: (i, 0)
    return pl.pallas_call(
        functools.partial(_ffn_kernel, f_chunk=256),
        out_shape=jax.ShapeDtypeStruct((t, d), F32),
        grid=(t // tm,),
        in_specs=[pl.BlockSpec((tm, d), row), _resident((d, f)), _resident((d, f)), _resident((f, d)),
                  pl.BlockSpec((tm, d), row)],
        out_specs=pl.BlockSpec((tm, d), row),
        scratch_shapes=[pltpu.VMEM((tm, f), BF16)],
        compiler_params=_params(("parallel",), est),
        name="dense_swiglu",
    )(hn, wg, wu, wd, res)


def _kvq_kernel(h_ref, gkv_ref, gq_ref, wkv_ref, wq_ref, kv_ref, q_ref):
    h = h_ref[...]
    y = h * _rms_scale(h)
    kv_ref[...] = _dot((y * gkv_ref[...]).astype(BF16), wkv_ref[...]).astype(BF16)
    q_ref[...] = _dot((y * gq_ref[...]).astype(BF16), wq_ref[...]).astype(BF16)


def _kvq(h, gkv, gq, wkv, wq):
    t, d = h.shape
    nkv, nq = wkv.shape[1], wq.shape[1]
    tm = TOKEN_TILE
    est = 2 * (tm * d * 4 + d * (nkv + nq) * 2 + tm * (nkv + nq) * 2) + 4 * tm * d * 4
    row = lambda i: (i, 0)
    return pl.pallas_call(
        _kvq_kernel,
        out_shape=(jax.ShapeDtypeStruct((t, nkv), BF16), jax.ShapeDtypeStruct((t, nq), BF16)),
        grid=(t // tm,),
        in_specs=[pl.BlockSpec((tm, d), row), _resident((1, d)), _resident((1, d)), _resident((d, nkv)),
                  _resident((d, nq))],
        out_specs=(pl.BlockSpec((tm, nkv), row), pl.BlockSpec((tm, nq), row)),
        compiler_params=_params(("parallel",), est),
        name="kv_q_proj",
    )(h, gkv, gq, wkv, wq)


def _sb_kernel(q_ref, k_ref, v_ref, cs_ref, o_ref, acc_ref, carry_ref, *, scale):
    tq = q_ref.shape[0]
    i = pl.program_id(2)
    q = q_ref[...]
    cs = cs_ref[...]
    acc_ref[...] = jnp.zeros_like(acc_ref)
    carry_ref[...] = jnp.zeros_like(carry_ref)

    def block(jb, diagonal):
        start = pl.multiple_of(jb * tq, tq)
        kb = k_ref[pl.ds(start, tq), :]
        vb = v_ref[pl.ds(start, tq), :]
        z = _dot_nt(q, kb) * scale
        log_beta = _log_sigmoid(z)
        log_keep = log_beta - z
        if diagonal:
            causal = (lax.broadcasted_iota(jnp.int32, (tq, tq), 1)
                      < lax.broadcasted_iota(jnp.int32, (tq, tq), 0))
            log_keep = jnp.where(causal, log_keep, 0.0)
        hi, lo = _split_bf16(log_keep)
        carry = carry_ref[...]
        weights = [None] * (tq // LANES)
        for sub in reversed(range(tq // LANES)):
            cols = slice(sub * LANES, (sub + 1) * LANES)
            sums = _dot(jnp.concatenate([hi[:, cols], lo[:, cols]], axis=1), cs)
            w = jnp.exp(log_beta[:, cols] + carry + sums[:, :LANES])
            if diagonal:
                w = jnp.where(causal[:, cols], w, 0.0)
            weights[sub] = w.astype(BF16)
            carry = carry + sums[:, LANES:]
        carry_ref[...] = carry
        acc_ref[...] += _dot(jnp.concatenate(weights, axis=1), vb)

    block(i, True)

    def body(n, c):
        block(i - 1 - n, False)
        return c

    lax.fori_loop(0, i, body, 0)
    o_ref[...] = acc_ref[...].astype(o_ref.dtype)


def _sb_cumsum_matrix():
    j = jnp.arange(LANES)
    later = (j[:, None] > j[None, :]).astype(BF16)
    half = jnp.concatenate([later, jnp.ones((LANES, LANES), BF16)], axis=1)
    return jnp.concatenate([half, half], axis=0)


def _sb_attention(q, kv, b_sz, s_len, heads, hd):
    tq = SB_TILE
    q = q.reshape(b_sz, s_len, heads * hd)
    kv = kv.reshape(b_sz, s_len, 2 * heads * hd)
    cs = _sb_cumsum_matrix()
    est = 2 * (tq * hd * 2 * 2 + 2 * s_len * hd * 2) + 2 * tq * LANES * 4 + 12 * tq * tq * 4
    out = pl.pallas_call(
        functools.partial(_sb_kernel, scale=hd ** -0.5),
        out_shape=jax.ShapeDtypeStruct((b_sz, s_len, heads * hd), BF16),
        grid=(b_sz, heads, s_len // tq),
        in_specs=[pl.BlockSpec((None, tq, hd), lambda b, h, i: (b, i, h)),
                  pl.BlockSpec((None, s_len, hd), lambda b, h, i: (b, 0, h)),
                  pl.BlockSpec((None, s_len, hd), lambda b, h, i: (b, 0, heads + h)),
                  _resident(cs.shape)],
        out_specs=pl.BlockSpec((None, tq, hd), lambda b, h, i: (b, i, h)),
        scratch_shapes=[pltpu.VMEM((tq, hd), F32), pltpu.VMEM((tq, LANES), F32)],
        compiler_params=_params(("parallel", "parallel", "arbitrary"), est),
        name="stick_breaking_attention",
    )(q, kv, kv, cs)
    return out.reshape(b_sz * s_len, heads * hd)


def _route_kernel(o_ref, w_ref, res_ref, g_ref, wr_hi_ref, wr_lo_ref, lt_ref,
                  h_ref, hn_ref, info_ref, cnt_ref, run_ref):
    @pl.when(pl.program_id(0) == 0)
    def _():
        run_ref[...] = jnp.zeros_like(run_ref)

    h = res_ref[...] + _dot(o_ref[...], w_ref[...])
    h_ref[...] = h
    hn = h * _rms_scale(h) * g_ref[...]
    hn_ref[...] = hn
    a_hi, a_lo = _split_bf16(hn)
    logits = _dot(a_hi, wr_hi_ref[...]) + (_dot(a_lo, wr_hi_ref[...]) + _dot(a_hi, wr_lo_ref[...]))
    lane = lax.broadcasted_iota(jnp.int32, logits.shape, 1).astype(F32)
    neg = jnp.float32(-jnp.inf)
    logits = jnp.where(lane < N_EXPERTS, logits, neg)
    v1 = jnp.max(logits, axis=-1, keepdims=True)
    i1 = jnp.min(jnp.where(logits == v1, lane, float(LANES)), axis=-1, keepdims=True)
    rest = jnp.where(lane == i1, neg, logits)
    v2 = jnp.max(rest, axis=-1, keepdims=True)
    i2 = jnp.min(jnp.where(rest == v2, lane, float(LANES)), axis=-1, keepdims=True)
    e = jnp.exp(v2 - v1)
    g1 = 1.0 / (1.0 + e)
    g2 = e / (1.0 + e)
    hot1 = lane == i1
    hot2 = lane == i2
    member = jnp.where(jnp.logical_or(hot1, hot2), 1.0, 0.0)
    before = _dot(lt_ref[...], member.astype(BF16)) + run_ref[0:1, :]
    rank1 = jnp.sum(jnp.where(hot1, before, 0.0), axis=-1, keepdims=True)
    rank2 = jnp.sum(jnp.where(hot2, before, 0.0), axis=-1, keepdims=True)
    total = run_ref[0:1, :] + jnp.sum(member, axis=0, keepdims=True)
    run_ref[...] = jnp.broadcast_to(total, run_ref.shape)
    cnt_ref[...] = jnp.broadcast_to(total, cnt_ref.shape)
    info = jnp.where(lane == 0, i1.astype(F32), 0.0)
    info = jnp.where(lane == 1, i2.astype(F32), info)
    info = jnp.where(lane == 2, g1, info)
    info = jnp.where(lane == 3, g2, info)
    info = jnp.where(lane == 4, rank1, info)
    info = jnp.where(lane == 5, rank2, info)
    info_ref[...] = info


def _route(o, w, res, gain, wr_hi, wr_lo):
    t, d = res.shape
    k = o.shape[1]
    tm = TOKEN_TILE
    lt = jnp.tril(jnp.ones((tm, tm), BF16), -1)
    est = 2 * (tm * k * 2 + k * d * 2 + 3 * tm * d * 4 + 2 * d * LANES * 2 + tm * tm * 2 + tm * LANES * 4) + 6 * tm * d * 4
    row = lambda i: (i, 0)
    return pl.pallas_call(
        _route_kernel,
        out_shape=(jax.ShapeDtypeStruct((t, d), F32), jax.ShapeDtypeStruct((t, d), F32),
                   jax.ShapeDtypeStruct((t, LANES), F32), jax.ShapeDtypeStruct((8, LANES), F32)),
        grid=(t // tm,),
        in_specs=[pl.BlockSpec((tm, k), row), _resident((k, d)), pl.BlockSpec((tm, d), row), _resident((1, d)),
                  _resident((d, LANES)), _resident((d, LANES)), _resident((tm, tm))],
        out_specs=(pl.BlockSpec((tm, d), row), pl.BlockSpec((tm, d), row), pl.BlockSpec((tm, LANES), row),
                   _resident((8, LANES))),
        scratch_shapes=[pltpu.VMEM((8, LANES), F32)],
        compiler_params=_params(("arbitrary",), est),
        name="outproj_norm_router",
    )(o, w, res, gain, wr_hi, wr_lo, lt)


def _scatter_kernel(pos1_ref, pos2_ref, src_ref, init_ref, dst_ref, sem):
    del init_ref
    tile = pos1_ref.shape[0]
    base = pl.program_id(0) * tile

    def copies(r):
        src = src_ref.at[pl.ds(base + r, 1)]
        return (pltpu.make_async_copy(src, dst_ref.at[pl.ds(pos1_ref[r], 1)], sem),
                pltpu.make_async_copy(src, dst_ref.at[pl.ds(pos2_ref[r], 1)], sem))

    def start(r, c):
        for cp in copies(r):
            cp.start()
        return c

    def wait(r, c):
        for cp in copies(r):
            cp.wait()
        return c

    lax.fori_loop(0, tile, start, 0)
    lax.fori_loop(0, tile, wait, 0)


def _scatter_rows(pos1, pos2, src, n_sorted):
    t, d = src.shape
    tile = ROW_TILE
    init = jnp.zeros((n_sorted, d), src.dtype)
    smem = lambda: pl.BlockSpec((tile,), lambda i: (i,), memory_space=pltpu.SMEM)
    return pl.pallas_call(
        _scatter_kernel,
        out_shape=jax.ShapeDtypeStruct((n_sorted, d), src.dtype),
        grid=(t // tile,),
        in_specs=[smem(), smem(), pl.BlockSpec(memory_space=pl.ANY), pl.BlockSpec(memory_space=pl.ANY)],
        out_specs=pl.BlockSpec(memory_space=pl.ANY),
        scratch_shapes=[pltpu.SemaphoreType.DMA(())],
        input_output_aliases={3: 0},
        compiler_params=pltpu.CompilerParams(dimension_semantics=("arbitrary",), has_side_effects=True),
        name="moe_row_scatter",
    )(pos1, pos2, src, init)


def _moe_kernel(te_ref, nv_ref, x_ref, wg_ref, wu_ref, wd_ref, y_ref, xb_ref, acc_ref):
    del te_ref
    j = pl.program_id(0)
    f = pl.program_id(1)

    @pl.when(j < nv_ref[0])
    def _():
        @pl.when(f == 0)
        def _():
            xb_ref[...] = x_ref[...].astype(BF16)
            acc_ref[...] = jnp.zeros_like(acc_ref)

        xb = xb_ref[...]
        act = (_silu(_dot(xb, wg_ref[...])) * _dot(xb, wu_ref[...])).astype(BF16)
        acc_ref[...] += _dot(act, wd_ref[...])

        @pl.when(f == pl.num_programs(1) - 1)
        def _():
            y_ref[...] = acc_ref[...]

    @pl.when(jnp.logical_and(j >= nv_ref[0], f == pl.num_programs(1) - 1))
    def _():
        y_ref[...] = jnp.zeros_like(y_ref)


def _moe(tile_expert, n_valid, xs, wg, wu, wd):
    p, d = xs.shape
    f_dim = wg.shape[2]
    tm, tf = MOE_TILE, MOE_F_TILE
    nf = f_dim // tf

    def tile_idx(j, nv):
        return jnp.minimum(j, nv[0] - 1)

    def f_idx(j, f, nv):
        return jnp.where(j < nv[0], f, nf - 1)

    grid_spec = pltpu.PrefetchScalarGridSpec(
        num_scalar_prefetch=2,
        grid=(p // tm, nf),
        in_specs=[pl.BlockSpec((tm, d), lambda j, f, te, nv: (tile_idx(j, nv), 0)),
                  pl.BlockSpec((None, d, tf), lambda j, f, te, nv: (te[tile_idx(j, nv)], 0, f_idx(j, f, nv))),
                  pl.BlockSpec((None, d, tf), lambda j, f, te, nv: (te[tile_idx(j, nv)], 0, f_idx(j, f, nv))),
                  pl.BlockSpec((None, tf, d), lambda j, f, te, nv: (te[tile_idx(j, nv)], f_idx(j, f, nv), 0))],
        out_specs=pl.BlockSpec((tm, d), lambda j, f, te, nv: (j, 0)),
        scratch_shapes=[pltpu.VMEM((tm, d), BF16), pltpu.VMEM((tm, d), F32)])
    est = 2 * (2 * tm * d * 4 + 3 * d * tf * 2) + tm * d * 6 + 6 * tm * tf * 4
    return pl.pallas_call(
        _moe_kernel,
        out_shape=jax.ShapeDtypeStruct((p, d), F32),
        grid_spec=grid_spec,
        compiler_params=_params(("arbitrary", "arbitrary"), est),
        name="moe_grouped_swiglu",
    )(tile_expert, n_valid, xs, wg, wu, wd)


def _combine_kernel(pos1_ref, pos2_ref, h_ref, info_ref, g_ref, y_ref, out_ref, buf_ref, sem):
    tile = h_ref.shape[0]

    def copies(r):
        return (pltpu.make_async_copy(y_ref.at[pl.ds(pos1_ref[r], 1)], buf_ref.at[0, pl.ds(r, 1)], sem),
                pltpu.make_async_copy(y_ref.at[pl.ds(pos2_ref[r], 1)], buf_ref.at[1, pl.ds(r, 1)], sem))

    def start(r, c):
        for cp in copies(r):
            cp.start()
        return c

    def wait(r, c):
        for cp in copies(r):
            cp.wait()
        return c

    lax.fori_loop(0, tile, start, 0)
    lax.fori_loop(0, tile, wait, 0)
    info = info_ref[...]
    h = h_ref[...] + (info[:, 2:3] * buf_ref[0] + info[:, 3:4] * buf_ref[1])
    out_ref[...] = h * _rms_scale(h) * g_ref[...]


def _combine(pos1, pos2, h, info, gain, y):
    t, d = h.shape
    tile = ROW_TILE
    smem = lambda: pl.BlockSpec((tile,), lambda i: (i,), memory_space=pltpu.SMEM)
    row = lambda i: (i, 0)
    est = 2 * (2 * tile * d * 4 + tile * LANES * 4) + 2 * tile * d * 4 + 4 * tile * d * 4
    return pl.pallas_call(
        _combine_kernel,
        out_shape=jax.ShapeDtypeStruct((t, d), F32),
        grid=(t // tile,),
        in_specs=[smem(), smem(), pl.BlockSpec((tile, d), row), pl.BlockSpec((tile, LANES), row), _resident((1, d)),
                  pl.BlockSpec(memory_space=pl.ANY)],
        out_specs=pl.BlockSpec((tile, d), row),
        scratch_shapes=[pltpu.VMEM((2, tile, d), F32), pltpu.SemaphoreType.DMA(())],
        compiler_params=_params(("arbitrary",), est),
        name="moe_combine_final_norm",
    )(pos1, pos2, h, info, gain, y)


def kernel(x, attn_norm, ffn_norm, kv_norm, final_norm, gla_w_in, gla_w_gate_up, gla_b_gate, gla_head_norm,
           gla_w_out, sb_w_kv, sb_w_q, sb_w_out, ffn_w_gate, ffn_w_up, ffn_w_down, moe_w_router, moe_w_gate,
           moe_w_up, moe_w_down):
    b_sz, s_len, d = x.shape
    t = b_sz * s_len
    heads = GLA_HEADS
    d_qk = gla_w_gate_up.shape[2] * 2
    hk = d_qk // 2
    rank = gla_w_gate_up.shape[1]
    d_v = gla_w_out.shape[1]
    dk, dv = hk // heads, d_v // heads
    assert gla_w_in.shape[2] == d_qk + 2 * d_v + rank and rank <= LANES
    assert t % TOKEN_TILE == 0 and s_len % GLA_TILE == 0 and s_len % SB_TILE == 0
    assert attn_norm.shape[0] == 2 and moe_w_gate.shape[1] == N_EXPERTS

    xf = x.reshape(t, d)
    gain = lambda g: g.reshape(1, -1).astype(F32)

    w_in = gla_w_in[0]
    c_v, c_g, c_r = d_qk, d_qk + d_v, d_qk + d_v + rank
    w_in_cat = jnp.concatenate(
        [w_in[:, :c_v], w_in[:, c_v:c_g], w_in[:, c_r:], jnp.pad(w_in[:, c_g:c_r], ((0, 0), (0, LANES - rank)))],
        axis=1).astype(BF16)
    qk, v, r, glr = _inproj(xf, gain(attn_norm[0]), w_in_cat, d_qk, d_v)
    wgu = jnp.pad(gla_w_gate_up[0], ((0, LANES - rank), (0, 0))).astype(BF16)
    o = _gla(qk, v, r, glr, wgu, gla_b_gate[0].reshape(1, hk), gain(gla_head_norm[0]), b_sz, s_len, heads, dk, dv)
    h, hn = _outproj(o, gla_w_out[0].astype(BF16), xf, gain(ffn_norm[0]))

    h = _ffn(hn, ffn_w_gate[0].astype(BF16), ffn_w_up[0].astype(BF16), ffn_w_down[0].astype(BF16), h)

    kv, q = _kvq(h, gain(kv_norm), gain(attn_norm[1]), sb_w_kv.astype(BF16), sb_w_q[0].astype(BF16))
    hd = sb_w_q.shape[2] // SB_HEADS
    o = _sb_attention(q, kv, b_sz, s_len, SB_HEADS, hd)

    wr = jnp.pad(moe_w_router[0], ((0, 0), (0, LANES - N_EXPERTS)))
    wr_hi = wr.astype(BF16)
    wr_lo = (wr - wr_hi.astype(F32)).astype(BF16)
    h, hn, info, counts = _route(o, sb_w_out[0].astype(BF16), h, gain(ffn_norm[1]), wr_hi, wr_lo)

    cnt = counts[0, :N_EXPERTS].astype(jnp.int32)
    padded = (cnt + MOE_TILE - 1) // MOE_TILE * MOE_TILE
    ends = jnp.cumsum(padded)
    offs = ends - padded
    e1, e2 = info[:, 0].astype(jnp.int32), info[:, 1].astype(jnp.int32)
    pos1 = offs[e1] + info[:, 4].astype(jnp.int32)
    pos2 = offs[e2] + info[:, 5].astype(jnp.int32)
    n_sorted = 2 * t + N_EXPERTS * MOE_TILE
    n_tiles = n_sorted // MOE_TILE
    tile_start = jnp.arange(n_tiles, dtype=jnp.int32) * MOE_TILE
    tile_expert = jnp.minimum(jnp.sum(tile_start[:, None] >= ends[None, :], axis=1), N_EXPERTS - 1).astype(jnp.int32)
    n_valid = (ends[-1] // MOE_TILE).astype(jnp.int32).reshape(1)

    xs = _scatter_rows(pos1, pos2, hn, n_sorted)
    y = _moe(tile_expert, n_valid, xs, moe_w_gate[0].astype(BF16), moe_w_up[0].astype(BF16),
             moe_w_down[0].astype(BF16))
    out = _combine(pos1, pos2, h, info, gain(final_norm), y)
    return out.reshape(b_sz, s_len, d)
```

```python
import functools

import jax
import jax.numpy as jnp
from jax import lax
from jax.experimental import pallas as pl
from jax.experimental.pallas import tpu as pltpu

F32 = jnp.float32
BF16 = jnp.bfloat16

RMS_EPS = 1e-6
LOG2_E = 1.4426950408889634
GLA_HEADS = 4
GLA_GATE_NORM = 16.0
GLA_CHUNK = 64
SB_HEADS = 8
N_EXPERTS = 8

LANES = 128
V7X_VMEM_LIMIT_CAP = 60 * 1024 * 1024

TOKEN_TILE = 512
GLA_TILE = 256
SB_Q_TILE = 512
SB_K_TILE = 256
SB_HEAD_GROUP = 8
MOE_TILE = 512
MOE_F_TILE = 1792
MOE_F_CHUNK = 256
ROW_TILE = 256


def _vmem_limit(estimate_bytes):
    return int(min(max(estimate_bytes * 5 // 4 + (4 << 20), 16 << 20), V7X_VMEM_LIMIT_CAP))


def _params(semantics, estimate_bytes):
    return pltpu.CompilerParams(dimension_semantics=semantics,
                                vmem_limit_bytes=_vmem_limit(estimate_bytes))


def _rms_scale(x):
    return lax.rsqrt(jnp.mean(x * x, axis=-1, keepdims=True) + RMS_EPS)


def _log_sigmoid(x):
    return jnp.minimum(x, 0.0) - jnp.log(1.0 + jnp.exp(-jnp.abs(x)))


def _silu(x):
    return x * (1.0 / (1.0 + jnp.exp(-x)))


def _dot(a, b):
    return jnp.dot(a, b, preferred_element_type=F32)


def _dot_nt(a, b):
    return lax.dot_general(a, b, (((1,), (1,)), ((), ())), preferred_element_type=F32)


def _dot_tn(a, b):
    return lax.dot_general(a, b, (((0,), (0,)), ((), ())), preferred_element_type=F32)


def _split_bf16(x):
    hi = x.astype(BF16)
    lo = (x - hi.astype(F32)).astype(BF16)
    return hi, lo


def _neg_abs(x):
    bits = lax.bitcast_convert_type(x, jnp.uint32) | jnp.uint32(0x80000000)
    return lax.bitcast_convert_type(bits, F32)


def _split_bf16_trunc(x):
    hi = lax.bitcast_convert_type(lax.bitcast_convert_type(x, jnp.uint32) & jnp.uint32(0xFFFF0000), F32)
    return hi.astype(BF16), (x - hi).astype(BF16)


def _resident(shape):
    return pl.BlockSpec(shape, lambda *_: (0,) * len(shape))


def _inproj_kernel(x_ref, g_ref, w_ref, qk_ref, v_ref, r_ref, glr_ref, *, d_qk, d_v):
    x = x_ref[...]
    xn = (x * _rms_scale(x) * g_ref[...]).astype(BF16)
    c0, c1, c2 = d_qk, d_qk + d_v, d_qk + 2 * d_v
    qk_ref[...] = _dot(xn, w_ref[:, 0:c0])
    v_ref[...] = _dot(xn, w_ref[:, c0:c1]).astype(BF16)
    r_ref[...] = _dot(xn, w_ref[:, c1:c2])
    glr_ref[...] = _dot(xn, w_ref[:, c2:c2 + LANES])


def _inproj(x, gain, w, d_qk, d_v):
    t, d = x.shape
    n = w.shape[1]
    tm = TOKEN_TILE
    est = 2 * (tm * d * 4 + d * n * 2 + tm * (d_qk * 4 + d_v * 2 + d_v * 4 + LANES * 4)) + 3 * tm * d * 4
    row = lambda i: (i, 0)
    return pl.pallas_call(
        functools.partial(_inproj_kernel, d_qk=d_qk, d_v=d_v),
        out_shape=(jax.ShapeDtypeStruct((t, d_qk), F32), jax.ShapeDtypeStruct((t, d_v), BF16),
                   jax.ShapeDtypeStruct((t, d_v), F32), jax.ShapeDtypeStruct((t, LANES), F32)),
        grid=(t // tm,),
        in_specs=[pl.BlockSpec((tm, d), row), _resident((1, d)), _resident((d, n))],
        out_specs=(pl.BlockSpec((tm, d_qk), row), pl.BlockSpec((tm, d_v), row),
                   pl.BlockSpec((tm, d_v), row), pl.BlockSpec((tm, LANES), row)),
        compiler_params=_params(("parallel",), est),
        name="gla_inproj",
    )(x, gain, w)


def _gla_kernel(qk_ref, v_ref, r_ref, glr_ref, wgu_ref, bg_ref, gh_ref, cum_ref, og_ref, st_ref,
                *, heads, dk, dv, chunk):
    tc = qk_ref.shape[0]
    hk = heads * dk

    @pl.when(pl.program_id(1) == 0)
    def _():
        st_ref[...] = jnp.zeros_like(st_ref)

    pre = _dot(glr_ref[...].astype(BF16), wgu_ref[...]) + bg_ref[...]
    log_a = _log_sigmoid(pre) * (1.0 / GLA_GATE_NORM)
    hi, lo = _split_bf16(log_a)
    cums = _dot(cum_ref[...], jnp.concatenate([hi, lo], axis=0))
    b = cums[:tc]
    b_end = cums[tc:]
    qk = qk_ref[...]
    q_dec = (qk[:, :hk] * (dk ** -0.5) * jnp.exp(b)).astype(BF16)
    k = qk[:, hk:]
    k_inv = (k * jnp.exp(-b)).astype(BF16)
    k_end = (k * jnp.exp(b_end - b)).astype(BF16)
    decay_end = jnp.exp(b_end)

    row = lax.broadcasted_iota(jnp.int32, (chunk, chunk), 0)
    col = lax.broadcasted_iota(jnp.int32, (chunk, chunk), 1)
    tri = col <= row
    gh = gh_ref[...]
    for c in range(tc // chunk):
        rows = slice(c * chunk, (c + 1) * chunk)
        for h in range(heads):
            kc = slice(h * dk, (h + 1) * dk)
            vc = slice(h * dv, (h + 1) * dv)
            qd = q_dec[rows, kc]
            vv = v_ref[rows, vc]
            st = st_ref[h]
            attn = jnp.where(tri, _dot_nt(qd, k_inv[rows, kc]), 0.0).astype(BF16)
            o = _dot_nt(qd, st.astype(BF16)) + _dot(attn, vv)
            st_ref[h] = decay_end[c * chunk:c * chunk + 1, kc] * st + _dot_tn(vv, k_end[rows, kc])
            o = o * _rms_scale(o) * gh
            og_ref[rows, vc] = (o * _silu(r_ref[rows, vc])).astype(BF16)


def _gla_cum_matrix(tc, chunk):
    i = jnp.arange(tc)
    same = (i[:, None] // chunk) == (i[None, :] // chunk)
    lower = jnp.logical_and(same, i[None, :] <= i[:, None])
    m = jnp.concatenate([lower, same], axis=0).astype(BF16)
    return jnp.concatenate([m, m], axis=1)


def _gla(qk, v, r, glr, wgu, bgate, ghead, b_sz, s_len, heads, dk, dv):
    tc = GLA_TILE
    hk, hv = heads * dk, heads * dv
    qk, v, r, glr = (a.reshape(b_sz, s_len, a.shape[-1]) for a in (qk, v, r, glr))
    cum = _gla_cum_matrix(tc, GLA_CHUNK)
    est = 2 * tc * (2 * hk * 4 + hv * 2 + hv * 4 + LANES * 4 + hv * 2) + heads * dv * dk * 4 + 16 * tc * hk * 4
    blk = lambda n: pl.BlockSpec((None, tc, n), lambda b, s: (b, s, 0))
    out = pl.pallas_call(
        functools.partial(_gla_kernel, heads=heads, dk=dk, dv=dv, chunk=GLA_CHUNK),
        out_shape=jax.ShapeDtypeStruct((b_sz, s_len, hv), BF16),
        grid=(b_sz, s_len // tc),
        in_specs=[blk(2 * hk), blk(hv), blk(hv), blk(LANES), _resident(wgu.shape), _resident(bgate.shape),
                  _resident(ghead.shape), _resident(cum.shape)],
        out_specs=blk(hv),
        scratch_shapes=[pltpu.VMEM((heads, dv, dk), F32)],
        compiler_params=_params(("parallel", "arbitrary"), est),
        name="gla_recurrence",
    )(qk, v, r, glr, wgu, bgate, ghead, cum)
    return out.reshape(b_sz * s_len, hv)


def _outproj_kernel(o_ref, w_ref, res_ref, g_ref, h_ref, hn_ref):
    h = res_ref[...] + _dot(o_ref[...], w_ref[...])
    h_ref[...] = h
    hn_ref[...] = (h * _rms_scale(h) * g_ref[...]).astype(BF16)


def _outproj(o, w, res, gain):
    t, d = res.shape
    k = o.shape[1]
    tm = TOKEN_TILE
    est = 2 * (tm * k * 2 + k * d * 2 + 2 * tm * d * 4 + tm * d * 2) + 2 * tm * d * 4
    row = lambda i: (i, 0)
    return pl.pallas_call(
        _outproj_kernel,
        out_shape=(jax.ShapeDtypeStruct((t, d), F32), jax.ShapeDtypeStruct((t, d), BF16)),
        grid=(t // tm,),
        in_specs=[pl.BlockSpec((tm, k), row), _resident((k, d)), pl.BlockSpec((tm, d), row), _resident((1, d))],
        out_specs=(pl.BlockSpec((tm, d), row), pl.BlockSpec((tm, d), row)),
        compiler_params=_params(("parallel",), est),
        name="outproj_residual_norm",
    )(o, w, res, gain)


def _ffn_kernel(hn_ref, wg_ref, wu_ref, wd_ref, res_ref, out_ref, act_ref, *, f_chunk):
    hn = hn_ref[...]
    f_dim = wg_ref.shape[1]
    for c0 in range(0, f_dim, f_chunk):
        cols = slice(c0, min(c0 + f_chunk, f_dim))
        act_ref[:, cols] = (_silu(_dot(hn, wg_ref[:, cols])) * _dot(hn, wu_ref[:, cols])).astype(BF16)
    out_ref[...] = res_ref[...] + _dot(act_ref[...], wd_ref[...])


def _ffn(hn, wg, wu, wd, res):
    t, d = res.shape
    f = wg.shape[1]
    tm = TOKEN_TILE
    est = 2 * (tm * d * 2 + 3 * d * f * 2 + 2 * tm * d * 4) + tm * f * 2 + 6 * tm * 256 * 4
    row = lambda i: (i, 0)
    return pl.pallas_call(
        functools.partial(_ffn_kernel, f_chunk=256),
        out_shape=jax.ShapeDtypeStruct((t, d), F32),
        grid=(t // tm,),
        in_specs=[pl.BlockSpec((tm, d), row), _resident((d, f)), _resident((d, f)), _resident((f, d)),
                  pl.BlockSpec((tm, d), row)],
        out_specs=pl.BlockSpec((tm, d), row),
        scratch_shapes=[pltpu.VMEM((tm, f), BF16)],
        compiler_params=_params(("parallel",), est),
        name="dense_swiglu",
    )(hn, wg, wu, wd, res)


def _kvq_kernel(h_ref, gkv_ref, gq_ref, wkv_ref, wq_ref, kv_ref, q_ref):
    h = h_ref[...]
    y = h * _rms_scale(h)
    kv_ref[...] = _dot((y * gkv_ref[...]).astype(BF16), wkv_ref[...]).astype(BF16)
    q_ref[...] = _dot((y * gq_ref[...]).astype(BF16), wq_ref[...]).astype(BF16)


def _kvq(h, gkv, gq, wkv, wq):
    t, d = h.shape
    nkv, nq = wkv.shape[1], wq.shape[1]
    tm = TOKEN_TILE
    est = 2 * (tm * d * 4 + d * (nkv + nq) * 2 + tm * (nkv + nq) * 2) + 4 * tm * d * 4
    row = lambda i: (i, 0)
    return pl.pallas_call(
        _kvq_kernel,
        out_shape=(jax.ShapeDtypeStruct((t, nkv), BF16), jax.ShapeDtypeStruct((t, nq), BF16)),
        grid=(t // tm,),
        in_specs=[pl.BlockSpec((tm, d), row), _resident((1, d)), _resident((1, d)), _resident((d, nkv)),
                  _resident((d, nq))],
        out_specs=(pl.BlockSpec((tm, nkv), row), pl.BlockSpec((tm, nq), row)),
        compiler_params=_params(("parallel",), est),
        name="kv_q_proj",
    )(h, gkv, gq, wkv, wq)


def _sb_kernel(q_ref, k_ref, v_ref, cs_ref, o_ref, acc_ref, carry_ref, *, scale, group, hd, tk):
    tq = q_ref.shape[0]
    i = pl.program_id(2)
    cs = cs_ref[...]
    acc_ref[...] = jnp.zeros_like(acc_ref)
    carry_ref[...] = jnp.zeros_like(carry_ref)

    def block(jb, masked):
        start = pl.multiple_of(jb * tk, tk)
        if masked:
            causal = (lax.broadcasted_iota(jnp.int32, (tq, tk), 1) + jb * tk
                      < lax.broadcasted_iota(jnp.int32, (tq, tk), 0) + i * tq)
        for g in range(group):
            hc = slice(g * hd, (g + 1) * hd)
            z = _dot_nt(q_ref[:, hc], k_ref[pl.ds(start, tk), hc]) * (scale * LOG2_E)
            log_beta = jnp.minimum(z, 0.0) - jnp.log(1.0 + jnp.exp2(_neg_abs(z))) * LOG2_E
            log_keep = log_beta - z
            if masked:
                log_keep = jnp.where(causal, log_keep, 0.0)
            hi, lo = _split_bf16_trunc(log_keep)
            carry = carry_ref[g]
            weights = [None] * (tk // LANES)
            for sub in reversed(range(tk // LANES)):
                cols = slice(sub * LANES, (sub + 1) * LANES)
                sums = _dot(jnp.concatenate([hi[:, cols], lo[:, cols]], axis=1), cs)
                w = jnp.exp2(log_beta[:, cols] + carry + sums[:, :LANES])
                if masked:
                    w = jnp.where(causal[:, cols], w, 0.0)
                weights[sub] = w.astype(BF16)
                carry = carry + sums[:, LANES:]
            carry_ref[g] = carry
            acc_ref[:, hc] += _dot(jnp.concatenate(weights, axis=1), v_ref[pl.ds(start, tk), hc])

    per = tq // tk
    for d in range(per):
        block((i + 1) * per - 1 - d, True)

    def body(n, c):
        block(i * per - 1 - n, False)
        return c

    lax.fori_loop(0, i * per, body, 0)
    o_ref[...] = acc_ref[...].astype(o_ref.dtype)


def _sb_cumsum_matrix():
    j = jnp.arange(LANES)
    later = (j[:, None] > j[None, :]).astype(BF16)
    half = jnp.concatenate([later, jnp.ones((LANES, LANES), BF16)], axis=1)
    return jnp.concatenate([half, half], axis=0)


def _sb_attention(q, kv, b_sz, s_len, heads, hd):
    tq, tk = SB_Q_TILE, SB_K_TILE
    q = q.reshape(b_sz, s_len, heads * hd)
    kv = kv.reshape(b_sz, s_len, 2 * heads * hd)
    cs = _sb_cumsum_matrix()
    group = SB_HEAD_GROUP
    gw = group * hd
    n_groups = heads // group
    est = 2 * (tq * gw * 2 * 2 + 2 * s_len * gw * 2) + tq * gw * 4 + group * tq * LANES * 4 + group * 12 * tq * tk * 4
    out = pl.pallas_call(
        functools.partial(_sb_kernel, scale=hd ** -0.5, group=group, hd=hd, tk=tk),
        out_shape=jax.ShapeDtypeStruct((b_sz, s_len, heads * hd), BF16),
        grid=(b_sz, n_groups, s_len // tq),
        in_specs=[pl.BlockSpec((None, tq, gw), lambda b, h, i: (b, i, h)),
                  pl.BlockSpec((None, s_len, gw), lambda b, h, i: (b, 0, h)),
                  pl.BlockSpec((None, s_len, gw), lambda b, h, i: (b, 0, n_groups + h)),
                  _resident(cs.shape)],
        out_specs=pl.BlockSpec((None, tq, gw), lambda b, h, i: (b, i, h)),
        scratch_shapes=[pltpu.VMEM((tq, gw), F32), pltpu.VMEM((group, tq, LANES), F32)],
        compiler_params=_params(("parallel", "parallel", "arbitrary"), est),
        name="stick_breaking_attention",
    )(q, kv, kv, cs)
    return out.reshape(b_sz * s_len, heads * hd)


def _route_kernel(o_ref, w_ref, res_ref, g_ref, wr_hi_ref, wr_lo_ref, lt_ref,
                  h_ref, hn_ref, info_ref, cnt_ref, run_ref):
    @pl.when(pl.program_id(0) == 0)
    def _():
        run_ref[...] = jnp.zeros_like(run_ref)

    h = res_ref[...] + _dot(o_ref[...], w_ref[...])
    h_ref[...] = h
    hn = h * _rms_scale(h) * g_ref[...]
    hn_ref[...] = hn
    a_hi, a_lo = _split_bf16(hn)
    logits = _dot(a_hi, wr_hi_ref[...]) + (_dot(a_lo, wr_hi_ref[...]) + _dot(a_hi, wr_lo_ref[...]))
    lane = lax.broadcasted_iota(jnp.int32, logits.shape, 1).astype(F32)
    neg = jnp.float32(-jnp.inf)
    logits = jnp.where(lane < N_EXPERTS, logits, neg)
    v1 = jnp.max(logits, axis=-1, keepdims=True)
    i1 = jnp.min(jnp.where(logits == v1, lane, float(LANES)), axis=-1, keepdims=True)
    rest = jnp.where(lane == i1, neg, logits)
    v2 = jnp.max(rest, axis=-1, keepdims=True)
    i2 = jnp.min(jnp.where(rest == v2, lane, float(LANES)), axis=-1, keepdims=True)
    e = jnp.exp(v2 - v1)
    g1 = 1.0 / (1.0 + e)
    g2 = e / (1.0 + e)
    hot1 = lane == i1
    hot2 = lane == i2
    member = jnp.where(jnp.logical_or(hot1, hot2), 1.0, 0.0)
    before = _dot(lt_ref[...], member.astype(BF16)) + run_ref[0:1, :]
    rank1 = jnp.sum(jnp.where(hot1, before, 0.0), axis=-1, keepdims=True)
    rank2 = jnp.sum(jnp.where(hot2, before, 0.0), axis=-1, keepdims=True)
    total = run_ref[0:1, :] + jnp.sum(member, axis=0, keepdims=True)
    run_ref[...] = jnp.broadcast_to(total, run_ref.shape)
    cnt_ref[...] = jnp.broadcast_to(total, cnt_ref.shape)
    info = jnp.where(lane == 0, i1.astype(F32), 0.0)
    info = jnp.where(lane == 1, i2.astype(F32), info)
    info = jnp.where(lane == 2, g1, info)
    info = jnp.where(lane == 3, g2, info)
    info = jnp.where(lane == 4, rank1, info)
    info = jnp.where(lane == 5, rank2, info)
    info_ref[...] = info


def _route(o, w, res, gain, wr_hi, wr_lo):
    t, d = res.shape
    k = o.shape[1]
    tm = TOKEN_TILE
    lt = jnp.tril(jnp.ones((tm, tm), BF16), -1)
    est = 2 * (tm * k * 2 + k * d * 2 + 3 * tm * d * 4 + 2 * d * LANES * 2 + tm * tm * 2 + tm * LANES * 4) + 6 * tm * d * 4
    row = lambda i: (i, 0)
    return pl.pallas_call(
        _route_kernel,
        out_shape=(jax.ShapeDtypeStruct((t, d), F32), jax.ShapeDtypeStruct((t, d), F32),
                   jax.ShapeDtypeStruct((t, LANES), F32), jax.ShapeDtypeStruct((8, LANES), F32)),
        grid=(t // tm,),
        in_specs=[pl.BlockSpec((tm, k), row), _resident((k, d)), pl.BlockSpec((tm, d), row), _resident((1, d)),
                  _resident((d, LANES)), _resident((d, LANES)), _resident((tm, tm))],
        out_specs=(pl.BlockSpec((tm, d), row), pl.BlockSpec((tm, d), row), pl.BlockSpec((tm, LANES), row),
                   _resident((8, LANES))),
        scratch_shapes=[pltpu.VMEM((8, LANES), F32)],
        compiler_params=_params(("arbitrary",), est),
        name="outproj_norm_router",
    )(o, w, res, gain, wr_hi, wr_lo, lt)


def _scatter_kernel(pos1_ref, pos2_ref, src_ref, init_ref, dst_ref, sem):
    del init_ref
    tile = pos1_ref.shape[0]

    def copies(r):
        src = src_ref.at[pl.ds(r, 1)]
        return (pltpu.make_async_copy(src, dst_ref.at[pl.ds(pos1_ref[r], 1)], sem),
                pltpu.make_async_copy(src, dst_ref.at[pl.ds(pos2_ref[r], 1)], sem))

    def start(r, c):
        for cp in copies(r):
            cp.start()
        return c

    def wait(r, c):
        for cp in copies(r):
            cp.wait()
        return c

    lax.fori_loop(0, tile, start, 0)
    lax.fori_loop(0, tile, wait, 0)


def _scatter_rows(pos1, pos2, src, n_sorted):
    t, d = src.shape
    tile = ROW_TILE
    init = jnp.zeros((n_sorted, d), src.dtype)
    smem = lambda: pl.BlockSpec((tile,), lambda i: (i,), memory_space=pltpu.SMEM)
    return pl.pallas_call(
        _scatter_kernel,
        out_shape=jax.ShapeDtypeStruct((n_sorted, d), src.dtype),
        grid=(t // tile,),
        in_specs=[smem(), smem(), pl.BlockSpec((tile, d), lambda i: (i, 0)), pl.BlockSpec(memory_space=pl.ANY)],
        out_specs=pl.BlockSpec(memory_space=pl.ANY),
        scratch_shapes=[pltpu.SemaphoreType.DMA(())],
        input_output_aliases={3: 0},
        compiler_params=pltpu.CompilerParams(dimension_semantics=("arbitrary",), has_side_effects=True,
                                             vmem_limit_bytes=_vmem_limit(2 * tile * d * 4)),
        name="moe_row_scatter",
    )(pos1, pos2, src, init)


def _moe_kernel(te_ref, nv_ref, x_ref, wg_ref, wu_ref, wd_ref, y_ref, xb_ref, act_ref, acc_ref, *, f_chunk):
    del te_ref
    j = pl.program_id(0)
    f = pl.program_id(1)
    nf = pl.num_programs(1)
    tf = wg_ref.shape[1]

    @pl.when(j < nv_ref[0])
    def _():
        @pl.when(f == 0)
        def _():
            xb_ref[...] = x_ref[...].astype(BF16)

        xb = xb_ref[...]
        for c0 in range(0, tf, f_chunk):
            cols = slice(c0, c0 + f_chunk)
            act_ref[:, cols] = (_silu(_dot(xb, wg_ref[:, cols])) * _dot(xb, wu_ref[:, cols])).astype(BF16)
        part = _dot(act_ref[...], wd_ref[...])

        @pl.when(jnp.logical_and(f == 0, nf > 1))
        def _():
            acc_ref[...] = part

        @pl.when(jnp.logical_and(f > 0, f < nf - 1))
        def _():
            acc_ref[...] += part

        @pl.when(jnp.logical_and(f == nf - 1, nf > 1))
        def _():
            y_ref[...] = acc_ref[...] + part

        @pl.when(nf == 1)
        def _():
            y_ref[...] = part

    @pl.when(jnp.logical_and(j >= nv_ref[0], f == nf - 1))
    def _():
        y_ref[...] = jnp.zeros_like(y_ref)


def _moe_f_tile(f_dim):
    nf = 1
    while f_dim % nf or f_dim // nf > MOE_F_TILE or (f_dim // nf) % MOE_F_CHUNK:
        nf += 1
    return f_dim // nf


def _moe(tile_expert, n_valid, xs, wg, wu, wd):
    p, d = xs.shape
    f_dim = wg.shape[2]
    tm, tf = MOE_TILE, _moe_f_tile(f_dim)
    nf = f_dim // tf

    def tile_idx(j, nv):
        return jnp.minimum(j, nv[0] - 1)

    def f_idx(j, f, nv):
        return jnp.where(j < nv[0], f, nf - 1)

    grid_spec = pltpu.PrefetchScalarGridSpec(
        num_scalar_prefetch=2,
        grid=(p // tm, nf),
        in_specs=[pl.BlockSpec((tm, d), lambda j, f, te, nv: (tile_idx(j, nv), 0)),
                  pl.BlockSpec((None, d, tf), lambda j, f, te, nv: (te[tile_idx(j, nv)], 0, f_idx(j, f, nv))),
                  pl.BlockSpec((None, d, tf), lambda j, f, te, nv: (te[tile_idx(j, nv)], 0, f_idx(j, f, nv))),
                  pl.BlockSpec((None, tf, d), lambda j, f, te, nv: (te[tile_idx(j, nv)], f_idx(j, f, nv), 0))],
        out_specs=pl.BlockSpec((tm, d), lambda j, f, te, nv: (j, 0)),
        scratch_shapes=[pltpu.VMEM((tm, d), BF16), pltpu.VMEM((tm, tf), BF16), pltpu.VMEM((tm, d), F32)])
    est = 2 * (2 * tm * d * 4 + 3 * d * tf * 2) + tm * d * 6 + tm * tf * 2 + 6 * tm * MOE_F_CHUNK * 4 + tm * d * 4
    return pl.pallas_call(
        functools.partial(_moe_kernel, f_chunk=MOE_F_CHUNK),
        out_shape=jax.ShapeDtypeStruct((p, d), F32),
        grid_spec=grid_spec,
        compiler_params=_params(("arbitrary", "arbitrary"), est),
        name="moe_grouped_swiglu",
    )(tile_expert, n_valid, xs, wg, wu, wd)


def _combine_kernel(pos1_ref, pos2_ref, nxt1_ref, nxt2_ref, h_ref, info_ref, g_ref, y_ref, out_ref, buf_ref, sem):
    tile = h_ref.shape[0]
    i = pl.program_id(0)
    slot = lax.rem(i, 2)

    def copies(p1_ref, p2_ref, s, r):
        return (pltpu.make_async_copy(y_ref.at[pl.ds(p1_ref[r], 1)], buf_ref.at[s, 0, pl.ds(r, 1)], sem.at[s]),
                pltpu.make_async_copy(y_ref.at[pl.ds(p2_ref[r], 1)], buf_ref.at[s, 1, pl.ds(r, 1)], sem.at[s]))

    def start_all(p1_ref, p2_ref, s):
        def start(r, c):
            for cp in copies(p1_ref, p2_ref, s, r):
                cp.start()
            return c
        lax.fori_loop(0, tile, start, 0)

    @pl.when(i == 0)
    def _():
        start_all(pos1_ref, pos2_ref, slot)

    @pl.when(i + 1 < pl.num_programs(0))
    def _():
        start_all(nxt1_ref, nxt2_ref, 1 - slot)

    def wait(r, c):
        for cp in copies(pos1_ref, pos2_ref, slot, r):
            cp.wait()
        return c

    lax.fori_loop(0, tile, wait, 0)
    info = info_ref[...]
    h = h_ref[...] + (info[:, 2:3] * buf_ref[slot, 0] + info[:, 3:4] * buf_ref[slot, 1])
    out_ref[...] = h * _rms_scale(h) * g_ref[...]


def _combine(pos1, pos2, h, info, gain, y):
    t, d = h.shape
    tile = ROW_TILE
    n = t // tile
    smem = lambda step: pl.BlockSpec((tile,), lambda i: (jnp.minimum(i + step, n - 1),), memory_space=pltpu.SMEM)
    row = lambda i: (i, 0)
    est = 2 * (2 * tile * d * 4 + tile * LANES * 4) + 4 * tile * d * 4 + 4 * tile * d * 4
    return pl.pallas_call(
        _combine_kernel,
        out_shape=jax.ShapeDtypeStruct((t, d), F32),
        grid=(n,),
        in_specs=[smem(0), smem(0), smem(1), smem(1), pl.BlockSpec((tile, d), row), pl.BlockSpec((tile, LANES), row),
                  _resident((1, d)), pl.BlockSpec(memory_space=pl.ANY)],
        out_specs=pl.BlockSpec((tile, d), row),
        scratch_shapes=[pltpu.VMEM((2, 2, tile, d), F32), pltpu.SemaphoreType.DMA((2,))],
        compiler_params=_params(("arbitrary",), est),
        name="moe_combine_final_norm",
    )(pos1, pos2, pos1, pos2, h, info, gain, y)


def kernel(x, attn_norm, ffn_norm, kv_norm, final_norm, gla_w_in, gla_w_gate_up, gla_b_gate, gla_head_norm,
           gla_w_out, sb_w_kv, sb_w_q, sb_w_out, ffn_w_gate, ffn_w_up, ffn_w_down, moe_w_router, moe_w_gate,
           moe_w_up, moe_w_down):
    b_sz, s_len, d = x.shape
    t = b_sz * s_len
    heads = GLA_HEADS
    d_qk = gla_w_gate_up.shape[2] * 2
    hk = d_qk // 2
    rank = gla_w_gate_up.shape[1]
    d_v = gla_w_out.shape[1]
    dk, dv = hk // heads, d_v // heads
    assert gla_w_in.shape[2] == d_qk + 2 * d_v + rank and rank <= LANES
    assert t % TOKEN_TILE == 0 and s_len % GLA_TILE == 0 and s_len % SB_Q_TILE == 0
    assert attn_norm.shape[0] == 2 and moe_w_gate.shape[1] == N_EXPERTS

    xf = x.reshape(t, d)
    gain = lambda g: g.reshape(1, -1).astype(F32)

    w_in = gla_w_in[0]
    c_v, c_g, c_r = d_qk, d_qk + d_v, d_qk + d_v + rank
    w_in_cat = jnp.concatenate(
        [w_in[:, :c_v], w_in[:, c_v:c_g], w_in[:, c_r:], jnp.pad(w_in[:, c_g:c_r], ((0, 0), (0, LANES - rank)))],
        axis=1).astype(BF16)
    qk, v, r, glr = _inproj(xf, gain(attn_norm[0]), w_in_cat, d_qk, d_v)
    wgu = jnp.pad(gla_w_gate_up[0], ((0, LANES - rank), (0, 0))).astype(BF16)
    o = _gla(qk, v, r, glr, wgu, gla_b_gate[0].reshape(1, hk), gain(gla_head_norm[0]), b_sz, s_len, heads, dk, dv)
    h, hn = _outproj(o, gla_w_out[0].astype(BF16), xf, gain(ffn_norm[0]))

    h = _ffn(hn, ffn_w_gate[0].astype(BF16), ffn_w_up[0].astype(BF16), ffn_w_down[0].astype(BF16), h)

    kv, q = _kvq(h, gain(kv_norm), gain(attn_norm[1]), sb_w_kv.astype(BF16), sb_w_q[0].astype(BF16))
    hd = sb_w_q.shape[2] // SB_HEADS
    o = _sb_attention(q, kv, b_sz, s_len, SB_HEADS, hd)

    wr = jnp.pad(moe_w_router[0], ((0, 0), (0, LANES - N_EXPERTS)))
    wr_hi = wr.astype(BF16)
    wr_lo = (wr - wr_hi.astype(F32)).astype(BF16)
    h, hn, info, counts = _route(o, sb_w_out[0].astype(BF16), h, gain(ffn_norm[1]), wr_hi, wr_lo)

    cnt = counts[0, :N_EXPERTS].astype(jnp.int32)
    padded = (cnt + MOE_TILE - 1) // MOE_TILE * MOE_TILE
    ends = jnp.cumsum(padded)
    offs = ends - padded
    e1, e2 = info[:, 0].astype(jnp.int32), info[:, 1].astype(jnp.int32)
    pos1 = offs[e1] + info[:, 4].astype(jnp.int32)
    pos2 = offs[e2] + info[:, 5].astype(jnp.int32)
    n_sorted = 2 * t + N_EXPERTS * MOE_TILE
    n_tiles = n_sorted // MOE_TILE
    tile_start = jnp.arange(n_tiles, dtype=jnp.int32) * MOE_TILE
    tile_expert = jnp.minimum(jnp.sum(tile_start[:, None] >= ends[None, :], axis=1), N_EXPERTS - 1).astype(jnp.int32)
    n_valid = (ends[-1] // MOE_TILE).astype(jnp.int32).reshape(1)

    xs = _scatter_rows(pos1, pos2, hn, n_sorted)
    y = _moe(tile_expert, n_valid, xs, moe_w_gate[0].astype(BF16), moe_w_up[0].astype(BF16),
             moe_w_down[0].astype(BF16))
    out = _combine(pos1, pos2, h, info, gain(final_norm), y)
    return out.reshape(b_sz, s_len, d)
```

```python
import functools

import jax
import jax.numpy as jnp
from jax import lax
from jax.experimental import pallas as pl
from jax.experimental.pallas import tpu as pltpu

F32 = jnp.float32
BF16 = jnp.bfloat16

RMS_EPS = 1e-6
LOG2_E = 1.4426950408889634
GLA_HEADS = 4
GLA_GATE_NORM = 16.0
GLA_CHUNK = 64
SB_HEADS = 8
N_EXPERTS = 8

LANES = 128
V7X_VMEM_LIMIT_CAP = 60 * 1024 * 1024

TOKEN_TILE = 512
GLA_TILE = 256
GLA_BATCH = 2
SB_Q_TILE = 512
SB_K_TILE = 256
SB_HEAD_GROUP = 8
MOE_TILE = 512
MOE_F_TILE = 1792
MOE_F_CHUNK = 256
ROW_TILE = 256
ROW_UNROLL = 8


def _vmem_limit(estimate_bytes):
    return int(min(max(estimate_bytes * 5 // 4 + (4 << 20), 16 << 20), V7X_VMEM_LIMIT_CAP))


def _params(semantics, estimate_bytes):
    return pltpu.CompilerParams(dimension_semantics=semantics,
                                vmem_limit_bytes=_vmem_limit(estimate_bytes))


def _rms_scale(x):
    return lax.rsqrt(jnp.mean(x * x, axis=-1, keepdims=True) + RMS_EPS)


def _log_sigmoid(x):
    return jnp.minimum(x, 0.0) - jnp.log(1.0 + jnp.exp(-jnp.abs(x)))


def _silu(x):
    return x * (1.0 / (1.0 + jnp.exp(-x)))


def _dot(a, b):
    return jnp.dot(a, b, preferred_element_type=F32)


def _dot_nt(a, b):
    return lax.dot_general(a, b, (((1,), (1,)), ((), ())), preferred_element_type=F32)


def _dot_tn(a, b):
    return lax.dot_general(a, b, (((0,), (0,)), ((), ())), preferred_element_type=F32)


def _split_bf16(x):
    hi = x.astype(BF16)
    lo = (x - hi.astype(F32)).astype(BF16)
    return hi, lo


def _neg_abs(x):
    bits = lax.bitcast_convert_type(x, jnp.uint32) | jnp.uint32(0x80000000)
    return lax.bitcast_convert_type(bits, F32)


def _split_bf16_trunc(x):
    hi = lax.bitcast_convert_type(lax.bitcast_convert_type(x, jnp.uint32) & jnp.uint32(0xFFFF0000), F32)
    return hi.astype(BF16), (x - hi).astype(BF16)


def _resident(shape):
    return pl.BlockSpec(shape, lambda *_: (0,) * len(shape))


def _inproj_kernel(x_ref, g_ref, w_ref, qk_ref, v_ref, r_ref, glr_ref, *, d_qk, d_v):
    x = x_ref[...]
    xn = (x * _rms_scale(x) * g_ref[...]).astype(BF16)
    c0, c1, c2 = d_qk, d_qk + d_v, d_qk + 2 * d_v
    qk_ref[...] = _dot(xn, w_ref[:, 0:c0])
    v_ref[...] = _dot(xn, w_ref[:, c0:c1]).astype(BF16)
    r_ref[...] = _dot(xn, w_ref[:, c1:c2])
    glr_ref[...] = _dot(xn, w_ref[:, c2:c2 + LANES])


def _inproj(x, gain, w, d_qk, d_v):
    t, d = x.shape
    n = w.shape[1]
    tm = TOKEN_TILE
    est = 2 * (tm * d * 4 + d * n * 2 + tm * (d_qk * 4 + d_v * 2 + d_v * 4 + LANES * 4)) + 3 * tm * d * 4
    row = lambda i: (i, 0)
    return pl.pallas_call(
        functools.partial(_inproj_kernel, d_qk=d_qk, d_v=d_v),
        out_shape=(jax.ShapeDtypeStruct((t, d_qk), F32), jax.ShapeDtypeStruct((t, d_v), BF16),
                   jax.ShapeDtypeStruct((t, d_v), F32), jax.ShapeDtypeStruct((t, LANES), F32)),
        grid=(t // tm,),
        in_specs=[pl.BlockSpec((tm, d), row), _resident((1, d)), _resident((d, n))],
        out_specs=(pl.BlockSpec((tm, d_qk), row), pl.BlockSpec((tm, d_v), row),
                   pl.BlockSpec((tm, d_v), row), pl.BlockSpec((tm, LANES), row)),
        compiler_params=_params(("parallel",), est),
        name="gla_inproj",
    )(x, gain, w)


def _gla_kernel(qk_ref, v_ref, r_ref, glr_ref, wgu_ref, bg_ref, gh_ref, cum_ref, og_ref, st_ref,
                *, heads, dk, dv, chunk):
    nb, tc = qk_ref.shape[0], qk_ref.shape[1]
    hk = heads * dk

    @pl.when(pl.program_id(1) == 0)
    def _():
        st_ref[...] = jnp.zeros_like(st_ref)

    prepared = []
    for bb in range(nb):
        pre = _dot(glr_ref[bb].astype(BF16), wgu_ref[...]) + bg_ref[...]
        log_a = _log_sigmoid(pre) * (1.0 / GLA_GATE_NORM)
        hi, lo = _split_bf16(log_a)
        cums = _dot(cum_ref[...], jnp.concatenate([hi, lo], axis=0))
        b = cums[:tc]
        b_end = cums[tc:]
        qk = qk_ref[bb]
        q_dec = (qk[:, :hk] * (dk ** -0.5) * jnp.exp(b)).astype(BF16)
        k = qk[:, hk:]
        k_inv = (k * jnp.exp(-b)).astype(BF16)
        k_end = (k * jnp.exp(b_end - b)).astype(BF16)
        prepared.append((q_dec, k_inv, k_end, jnp.exp(b_end)))

    row = lax.broadcasted_iota(jnp.int32, (chunk, chunk), 0)
    col = lax.broadcasted_iota(jnp.int32, (chunk, chunk), 1)
    tri = col <= row
    gh = gh_ref[...]
    for c in range(tc // chunk):
        rows = slice(c * chunk, (c + 1) * chunk)
        for bb in range(nb):
            q_dec, k_inv, k_end, decay_end = prepared[bb]
            for h in range(heads):
                kc = slice(h * dk, (h + 1) * dk)
                vc = slice(h * dv, (h + 1) * dv)
                qd = q_dec[rows, kc]
                vv = v_ref[bb, rows, vc]
                st = st_ref[bb, h]
                attn = jnp.where(tri, _dot_nt(qd, k_inv[rows, kc]), 0.0).astype(BF16)
                o = _dot_nt(qd, st.astype(BF16)) + _dot(attn, vv)
                st_ref[bb, h] = decay_end[c * chunk:c * chunk + 1, kc] * st + _dot_tn(vv, k_end[rows, kc])
                o = o * _rms_scale(o) * gh
                og_ref[bb, rows, vc] = (o * _silu(r_ref[bb, rows, vc])).astype(BF16)


def _gla_cum_matrix(tc, chunk):
    i = jnp.arange(tc)
    same = (i[:, None] // chunk) == (i[None, :] // chunk)
    lower = jnp.logical_and(same, i[None, :] <= i[:, None])
    m = jnp.concatenate([lower, same], axis=0).astype(BF16)
    return jnp.concatenate([m, m], axis=1)


def _gla(qk, v, r, glr, wgu, bgate, ghead, b_sz, s_len, heads, dk, dv):
    tc = GLA_TILE
    hk, hv = heads * dk, heads * dv
    qk, v, r, glr = (a.reshape(b_sz, s_len, a.shape[-1]) for a in (qk, v, r, glr))
    cum = _gla_cum_matrix(tc, GLA_CHUNK)
    nb = GLA_BATCH if b_sz % GLA_BATCH == 0 else 1
    est = nb * (2 * tc * (2 * hk * 4 + hv * 2 + hv * 4 + LANES * 4 + hv * 2) + heads * dv * dk * 4 + 16 * tc * hk * 4)
    blk = lambda n: pl.BlockSpec((nb, tc, n), lambda b, s: (b, s, 0))
    out = pl.pallas_call(
        functools.partial(_gla_kernel, heads=heads, dk=dk, dv=dv, chunk=GLA_CHUNK),
        out_shape=jax.ShapeDtypeStruct((b_sz, s_len, hv), BF16),
        grid=(b_sz // nb, s_len // tc),
        in_specs=[blk(2 * hk), blk(hv), blk(hv), blk(LANES), _resident(wgu.shape), _resident(bgate.shape),
                  _resident(ghead.shape), _resident(cum.shape)],
        out_specs=blk(hv),
        scratch_shapes=[pltpu.VMEM((nb, heads, dv, dk), F32)],
        compiler_params=_params(("parallel", "arbitrary"), est),
        name="gla_recurrence",
    )(qk, v, r, glr, wgu, bgate, ghead, cum)
    return out.reshape(b_sz * s_len, hv)


def _outproj_kernel(o_ref, w_ref, res_ref, g_ref, h_ref, hn_ref):
    h = res_ref[...] + _dot(o_ref[...], w_ref[...])
    h_ref[...] = h
    hn_ref[...] = (h * _rms_scale(h) * g_ref[...]).astype(BF16)


def _outproj(o, w, res, gain):
    t, d = res.shape
    k = o.shape[1]
    tm = TOKEN_TILE
    est = 2 * (tm * k * 2 + k * d * 2 + 2 * tm * d * 4 + tm * d * 2) + 2 * tm * d * 4
    row = lambda i: (i, 0)
    return pl.pallas_call(
        _outproj_kernel,
        out_shape=(jax.ShapeDtypeStruct((t, d), F32), jax.ShapeDtypeStruct((t, d), BF16)),
        grid=(t // tm,),
        in_specs=[pl.BlockSpec((tm, k), row), _resident((k, d)), pl.BlockSpec((tm, d), row), _resident((1, d))],
        out_specs=(pl.BlockSpec((tm, d), row), pl.BlockSpec((tm, d), row)),
        compiler_params=_params(("parallel",), est),
        name="outproj_residual_norm",
    )(o, w, res, gain)


def _ffn_kernel(hn_ref, wg_ref, wu_ref, wd_ref, res_ref, out_ref, act_ref, *, f_chunk):
    hn = hn_ref[...]
    f_dim = wg_ref.shape[1]
    for c0 in range(0, f_dim, f_chunk):
        cols = slice(c0, min(c0 + f_chunk, f_dim))
        act_ref[:, cols] = (_silu(_dot(hn, wg_ref[:, cols])) * _dot(hn, wu_ref[:, cols])).astype(BF16)
    out_ref[...] = res_ref[...] + _dot(act_ref[...], wd_ref[...])


def _ffn(hn, wg, wu, wd, res):
    t, d = res.shape
    f = wg.shape[1]
    tm = TOKEN_TILE
    est = 2 * (tm * d * 2 + 3 * d * f * 2 + 2 * tm * d * 4) + tm * f * 2 + 6 * tm * 256 * 4
    row = lambda i: (i, 0)
    return pl.pallas_call(
        functools.partial(_ffn_kernel, f_chunk=256),
        out_shape=jax.ShapeDtypeStruct((t, d), F32),
        grid=(t // tm,),
        in_specs=[pl.BlockSpec((tm, d), row), _resident((d, f)), _resident((d, f)), _resident((f, d)),
                  pl.BlockSpec((tm, d), row)],
        out_specs=pl.BlockSpec((tm, d), row),
        scratch_shapes=[pltpu.VMEM((tm, f), BF16)],
        compiler_params=_params(("parallel",), est),
        name="dense_swiglu",
    )(hn, wg, wu, wd, res)


def _kvq_kernel(h_ref, gkv_ref, gq_ref, wkv_ref, wq_ref, kv_ref, q_ref):
    h = h_ref[...]
    y = h * _rms_scale(h)
    kv_ref[...] = _dot((y * gkv_ref[...]).astype(BF16), wkv_ref[...]).astype(BF16)
    q_ref[...] = _dot((y * gq_ref[...]).astype(BF16), wq_ref[...]).astype(BF16)


def _kvq(h, gkv, gq, wkv, wq):
    t, d = h.shape
    nkv, nq = wkv.shape[1], wq.shape[1]
    tm = TOKEN_TILE
    est = 2 * (tm * d * 4 + d * (nkv + nq) * 2 + tm * (nkv + nq) * 2) + 4 * tm * d * 4
    row = lambda i: (i, 0)
    return pl.pallas_call(
        _kvq_kernel,
        out_shape=(jax.ShapeDtypeStruct((t, nkv), BF16), jax.ShapeDtypeStruct((t, nq), BF16)),
        grid=(t // tm,),
        in_specs=[pl.BlockSpec((tm, d), row), _resident((1, d)), _resident((1, d)), _resident((d, nkv)),
                  _resident((d, nq))],
        out_specs=(pl.BlockSpec((tm, nkv), row), pl.BlockSpec((tm, nq), row)),
        compiler_params=_params(("parallel",), est),
        name="kv_q_proj",
    )(h, gkv, gq, wkv, wq)


def _sb_kernel(q_ref, k_ref, v_ref, cs_ref, o_ref, acc_ref, carry_ref, *, scale, group, hd, tk):
    tq = q_ref.shape[0]
    i = pl.program_id(2)
    cs = cs_ref[...]
    acc_ref[...] = jnp.zeros_like(acc_ref)
    carry_ref[...] = jnp.zeros_like(carry_ref)

    def block(jb, masked, r0=0):
        start = pl.multiple_of(jb * tk, tk)
        rows = slice(r0, tq)
        if masked:
            causal = (lax.broadcasted_iota(jnp.int32, (tq - r0, tk), 1) + jb * tk
                      < lax.broadcasted_iota(jnp.int32, (tq - r0, tk), 0) + (i * tq + r0))
        for g in range(group):
            hc = slice(g * hd, (g + 1) * hd)
            z = _dot_nt(q_ref[rows, hc], k_ref[pl.ds(start, tk), hc]) * (scale * LOG2_E)
            log_beta = jnp.minimum(z, 0.0) - jnp.log(1.0 + jnp.exp2(_neg_abs(z))) * LOG2_E
            log_keep = log_beta - z
            if masked:
                log_keep = jnp.where(causal, log_keep, 0.0)
            hi, lo = _split_bf16_trunc(log_keep)
            carry = carry_ref[g, rows]
            weights = [None] * (tk // LANES)
            for sub in reversed(range(tk // LANES)):
                cols = slice(sub * LANES, (sub + 1) * LANES)
                sums = _dot(jnp.concatenate([hi[:, cols], lo[:, cols]], axis=1), cs)
                w = jnp.exp2(log_beta[:, cols] + carry + sums[:, :LANES])
                if masked:
                    w = jnp.where(causal[:, cols], w, 0.0)
                weights[sub] = w.astype(BF16)
                carry = carry + sums[:, LANES:]
            carry_ref[g, rows] = carry
            acc_ref[rows, hc] += _dot(jnp.concatenate(weights, axis=1), v_ref[pl.ds(start, tk), hc])

    per = tq // tk
    for d in range(per):
        block((i + 1) * per - 1 - d, True, r0=(per - 1 - d) * tk)

    def body(n, c):
        block(i * per - 1 - n, False)
        return c

    lax.fori_loop(0, i * per, body, 0)
    o_ref[...] = acc_ref[...].astype(o_ref.dtype)


def _sb_cumsum_matrix():
    j = jnp.arange(LANES)
    later = (j[:, None] > j[None, :]).astype(BF16)
    half = jnp.concatenate([later, jnp.ones((LANES, LANES), BF16)], axis=1)
    return jnp.concatenate([half, half], axis=0)


def _sb_attention(q, kv, b_sz, s_len, heads, hd):
    tq, tk = SB_Q_TILE, SB_K_TILE
    q = q.reshape(b_sz, s_len, heads * hd)
    kv = kv.reshape(b_sz, s_len, 2 * heads * hd)
    cs = _sb_cumsum_matrix()
    group = SB_HEAD_GROUP
    gw = group * hd
    n_groups = heads // group
    est = 2 * (tq * gw * 2 * 2 + 2 * s_len * gw * 2) + tq * gw * 4 + group * tq * LANES * 4 + group * 12 * tq * tk * 4
    out = pl.pallas_call(
        functools.partial(_sb_kernel, scale=hd ** -0.5, group=group, hd=hd, tk=tk),
        out_shape=jax.ShapeDtypeStruct((b_sz, s_len, heads * hd), BF16),
        grid=(b_sz, n_groups, s_len // tq),
        in_specs=[pl.BlockSpec((None, tq, gw), lambda b, h, i: (b, i, h)),
                  pl.BlockSpec((None, s_len, gw), lambda b, h, i: (b, 0, h)),
                  pl.BlockSpec((None, s_len, gw), lambda b, h, i: (b, 0, n_groups + h)),
                  _resident(cs.shape)],
        out_specs=pl.BlockSpec((None, tq, gw), lambda b, h, i: (b, i, h)),
        scratch_shapes=[pltpu.VMEM((tq, gw), F32), pltpu.VMEM((group, tq, LANES), F32)],
        compiler_params=_params(("parallel", "parallel", "arbitrary"), est),
        name="stick_breaking_attention",
    )(q, kv, kv, cs)
    return out.reshape(b_sz * s_len, heads * hd)


def _route_kernel(o_ref, w_ref, res_ref, g_ref, wr_hi_ref, wr_lo_ref, lt_ref,
                  h_ref, hn_ref, info_ref, cnt_ref, run_ref):
    @pl.when(pl.program_id(0) == 0)
    def _():
        run_ref[...] = jnp.zeros_like(run_ref)

    h = res_ref[...] + _dot(o_ref[...], w_ref[...])
    h_ref[...] = h
    hn = h * _rms_scale(h) * g_ref[...]
    hn_ref[...] = hn
    a_hi, a_lo = _split_bf16(hn)
    logits = _dot(a_hi, wr_hi_ref[...]) + (_dot(a_lo, wr_hi_ref[...]) + _dot(a_hi, wr_lo_ref[...]))
    lane = lax.broadcasted_iota(jnp.int32, logits.shape, 1).astype(F32)
    neg = jnp.float32(-jnp.inf)
    logits = jnp.where(lane < N_EXPERTS, logits, neg)
    v1 = jnp.max(logits, axis=-1, keepdims=True)
    i1 = jnp.min(jnp.where(logits == v1, lane, float(LANES)), axis=-1, keepdims=True)
    rest = jnp.where(lane == i1, neg, logits)
    v2 = jnp.max(rest, axis=-1, keepdims=True)
    i2 = jnp.min(jnp.where(rest == v2, lane, float(LANES)), axis=-1, keepdims=True)
    e = jnp.exp(v2 - v1)
    g1 = 1.0 / (1.0 + e)
    g2 = e / (1.0 + e)
    hot1 = lane == i1
    hot2 = lane == i2
    member = jnp.where(jnp.logical_or(hot1, hot2), 1.0, 0.0)
    before = _dot(lt_ref[...], member.astype(BF16)) + run_ref[0:1, :]
    rank1 = jnp.sum(jnp.where(hot1, before, 0.0), axis=-1, keepdims=True)
    rank2 = jnp.sum(jnp.where(hot2, before, 0.0), axis=-1, keepdims=True)
    total = run_ref[0:1, :] + jnp.sum(member, axis=0, keepdims=True)
    run_ref[...] = jnp.broadcast_to(total, run_ref.shape)
    cnt_ref[...] = jnp.broadcast_to(total, cnt_ref.shape)
    info = jnp.where(lane == 0, i1.astype(F32), 0.0)
    info = jnp.where(lane == 1, i2.astype(F32), info)
    info = jnp.where(lane == 2, g1, info)
    info = jnp.where(lane == 3, g2, info)
    info = jnp.where(lane == 4, rank1, info)
    info = jnp.where(lane == 5, rank2, info)
    info_ref[...] = info


def _route(o, w, res, gain, wr_hi, wr_lo):
    t, d = res.shape
    k = o.shape[1]
    tm = TOKEN_TILE
    lt = jnp.tril(jnp.ones((tm, tm), BF16), -1)
    est = 2 * (tm * k * 2 + k * d * 2 + 3 * tm * d * 4 + 2 * d * LANES * 2 + tm * tm * 2 + tm * LANES * 4) + 6 * tm * d * 4
    row = lambda i: (i, 0)
    return pl.pallas_call(
        _route_kernel,
        out_shape=(jax.ShapeDtypeStruct((t, d), F32), jax.ShapeDtypeStruct((t, d), F32),
                   jax.ShapeDtypeStruct((t, LANES), F32), jax.ShapeDtypeStruct((8, LANES), F32)),
        grid=(t // tm,),
        in_specs=[pl.BlockSpec((tm, k), row), _resident((k, d)), pl.BlockSpec((tm, d), row), _resident((1, d)),
                  _resident((d, LANES)), _resident((d, LANES)), _resident((tm, tm))],
        out_specs=(pl.BlockSpec((tm, d), row), pl.BlockSpec((tm, d), row), pl.BlockSpec((tm, LANES), row),
                   _resident((8, LANES))),
        scratch_shapes=[pltpu.VMEM((8, LANES), F32)],
        compiler_params=_params(("arbitrary",), est),
        name="outproj_norm_router",
    )(o, w, res, gain, wr_hi, wr_lo, lt)


def _scatter_kernel(pos1_ref, pos2_ref, src_ref, init_ref, dst_ref, sem):
    del init_ref
    tile = pos1_ref.shape[0]

    def copies(r):
        src = src_ref.at[pl.ds(r, 1)]
        return (pltpu.make_async_copy(src, dst_ref.at[pl.ds(pos1_ref[r], 1)], sem),
                pltpu.make_async_copy(src, dst_ref.at[pl.ds(pos2_ref[r], 1)], sem))

    def start(r, c):
        for cp in copies(r):
            cp.start()
        return c

    def wait(r, c):
        for cp in copies(r):
            cp.wait()
        return c

    lax.fori_loop(0, tile, start, 0, unroll=ROW_UNROLL)
    lax.fori_loop(0, tile, wait, 0, unroll=ROW_UNROLL)


def _scatter_rows(pos1, pos2, src, n_sorted):
    t, d = src.shape
    tile = ROW_TILE
    init = jnp.zeros((n_sorted, d), src.dtype)
    smem = lambda: pl.BlockSpec((tile,), lambda i: (i,), memory_space=pltpu.SMEM)
    return pl.pallas_call(
        _scatter_kernel,
        out_shape=jax.ShapeDtypeStruct((n_sorted, d), src.dtype),
        grid=(t // tile,),
        in_specs=[smem(), smem(), pl.BlockSpec((tile, d), lambda i: (i, 0)), pl.BlockSpec(memory_space=pl.ANY)],
        out_specs=pl.BlockSpec(memory_space=pl.ANY),
        scratch_shapes=[pltpu.SemaphoreType.DMA(())],
        input_output_aliases={3: 0},
        compiler_params=pltpu.CompilerParams(dimension_semantics=("arbitrary",), has_side_effects=True,
                                             vmem_limit_bytes=_vmem_limit(2 * tile * d * 4)),
        name="moe_row_scatter",
    )(pos1, pos2, src, init)


def _moe_kernel(te_ref, nv_ref, x_ref, wg_ref, wu_ref, wd_ref, y_ref, xb_ref, act_ref, acc_ref, *, f_chunk):
    del te_ref
    j = pl.program_id(0)
    f = pl.program_id(1)
    nf = pl.num_programs(1)
    tf = wg_ref.shape[1]

    @pl.when(j < nv_ref[0])
    def _():
        @pl.when(f == 0)
        def _():
            xb_ref[...] = x_ref[...].astype(BF16)

        xb = xb_ref[...]
        for c0 in range(0, tf, f_chunk):
            cols = slice(c0, c0 + f_chunk)
            act_ref[:, cols] = (_silu(_dot(xb, wg_ref[:, cols])) * _dot(xb, wu_ref[:, cols])).astype(BF16)
        part = _dot(act_ref[...], wd_ref[...])

        @pl.when(jnp.logical_and(f == 0, nf > 1))
        def _():
            acc_ref[...] = part

        @pl.when(jnp.logical_and(f > 0, f < nf - 1))
        def _():
            acc_ref[...] += part

        @pl.when(jnp.logical_and(f == nf - 1, nf > 1))
        def _():
            y_ref[...] = acc_ref[...] + part

        @pl.when(nf == 1)
        def _():
            y_ref[...] = part

    @pl.when(jnp.logical_and(j >= nv_ref[0], f == nf - 1))
    def _():
        y_ref[...] = jnp.zeros_like(y_ref)


def _moe_f_tile(f_dim):
    nf = 1
    while f_dim % nf or f_dim // nf > MOE_F_TILE or (f_dim // nf) % MOE_F_CHUNK:
        nf += 1
    return f_dim // nf


def _moe(tile_expert, n_valid, xs, wg, wu, wd):
    p, d = xs.shape
    f_dim = wg.shape[2]
    tm, tf = MOE_TILE, _moe_f_tile(f_dim)
    nf = f_dim // tf

    def tile_idx(j, nv):
        return jnp.minimum(j, nv[0] - 1)

    def f_idx(j, f, nv):
        return jnp.where(j < nv[0], f, nf - 1)

    grid_spec = pltpu.PrefetchScalarGridSpec(
        num_scalar_prefetch=2,
        grid=(p // tm, nf),
        in_specs=[pl.BlockSpec((tm, d), lambda j, f, te, nv: (tile_idx(j, nv), 0)),
                  pl.BlockSpec((None, d, tf), lambda j, f, te, nv: (te[tile_idx(j, nv)], 0, f_idx(j, f, nv))),
                  pl.BlockSpec((None, d, tf), lambda j, f, te, nv: (te[tile_idx(j, nv)], 0, f_idx(j, f, nv))),
                  pl.BlockSpec((None, tf, d), lambda j, f, te, nv: (te[tile_idx(j, nv)], f_idx(j, f, nv), 0))],
        out_specs=pl.BlockSpec((tm, d), lambda j, f, te, nv: (j, 0)),
        scratch_shapes=[pltpu.VMEM((tm, d), BF16), pltpu.VMEM((tm, tf), BF16), pltpu.VMEM((tm, d), F32)])
    est = 2 * (2 * tm * d * 4 + 3 * d * tf * 2) + tm * d * 6 + tm * tf * 2 + 6 * tm * MOE_F_CHUNK * 4 + tm * d * 4
    return pl.pallas_call(
        functools.partial(_moe_kernel, f_chunk=MOE_F_CHUNK),
        out_shape=jax.ShapeDtypeStruct((p, d), F32),
        grid_spec=grid_spec,
        compiler_params=_params(("arbitrary", "arbitrary"), est),
        name="moe_grouped_swiglu",
    )(tile_expert, n_valid, xs, wg, wu, wd)


def _combine_kernel(pos1_ref, pos2_ref, nxt1_ref, nxt2_ref, h_ref, info_ref, g_ref, y_ref, out_ref, buf_ref, sem):
    tile = h_ref.shape[0]
    i = pl.program_id(0)
    slot = lax.rem(i, 2)

    def copies(p1_ref, p2_ref, s, r):
        return (pltpu.make_async_copy(y_ref.at[pl.ds(p1_ref[r], 1)], buf_ref.at[s, 0, pl.ds(r, 1)], sem.at[s]),
                pltpu.make_async_copy(y_ref.at[pl.ds(p2_ref[r], 1)], buf_ref.at[s, 1, pl.ds(r, 1)], sem.at[s]))

    def start_all(p1_ref, p2_ref, s):
        def start(r, c):
            for cp in copies(p1_ref, p2_ref, s, r):
                cp.start()
            return c
        lax.fori_loop(0, tile, start, 0, unroll=ROW_UNROLL)

    @pl.when(i == 0)
    def _():
        start_all(pos1_ref, pos2_ref, slot)

    @pl.when(i + 1 < pl.num_programs(0))
    def _():
        start_all(nxt1_ref, nxt2_ref, 1 - slot)

    def wait(r, c):
        for cp in copies(pos1_ref, pos2_ref, slot, r):
            cp.wait()
        return c

    lax.fori_loop(0, tile, wait, 0, unroll=ROW_UNROLL)
    info = info_ref[...]
    h = h_ref[...] + (info[:, 2:3] * buf_ref[slot, 0] + info[:, 3:4] * buf_ref[slot, 1])
    out_ref[...] = h * _rms_scale(h) * g_ref[...]


def _combine(pos1, pos2, h, info, gain, y):
    t, d = h.shape
    tile = ROW_TILE
    n = t // tile
    smem = lambda step: pl.BlockSpec((tile,), lambda i: (jnp.minimum(i + step, n - 1),), memory_space=pltpu.SMEM)
    row = lambda i: (i, 0)
    est = 2 * (2 * tile * d * 4 + tile * LANES * 4) + 4 * tile * d * 4 + 4 * tile * d * 4
    return pl.pallas_call(
        _combine_kernel,
        out_shape=jax.ShapeDtypeStruct((t, d), F32),
        grid=(n,),
        in_specs=[smem(0), smem(0), smem(1), smem(1), pl.BlockSpec((tile, d), row), pl.BlockSpec((tile, LANES), row),
                  _resident((1, d)), pl.BlockSpec(memory_space=pl.ANY)],
        out_specs=pl.BlockSpec((tile, d), row),
        scratch_shapes=[pltpu.VMEM((2, 2, tile, d), F32), pltpu.SemaphoreType.DMA((2,))],
        compiler_params=_params(("arbitrary",), est),
        name="moe_combine_final_norm",
    )(pos1, pos2, pos1, pos2, h, info, gain, y)


def kernel(x, attn_norm, ffn_norm, kv_norm, final_norm, gla_w_in, gla_w_gate_up, gla_b_gate, gla_head_norm,
           gla_w_out, sb_w_kv, sb_w_q, sb_w_out, ffn_w_gate, ffn_w_up, ffn_w_down, moe_w_router, moe_w_gate,
           moe_w_up, moe_w_down):
    b_sz, s_len, d = x.shape
    t = b_sz * s_len
    heads = GLA_HEADS
    d_qk = gla_w_gate_up.shape[2] * 2
    hk = d_qk // 2
    rank = gla_w_gate_up.shape[1]
    d_v = gla_w_out.shape[1]
    dk, dv = hk // heads, d_v // heads
    assert gla_w_in.shape[2] == d_qk + 2 * d_v + rank and rank <= LANES
    assert t % TOKEN_TILE == 0 and s_len % GLA_TILE == 0 and s_len % SB_Q_TILE == 0
    assert attn_norm.shape[0] == 2 and moe_w_gate.shape[1] == N_EXPERTS

    xf = x.reshape(t, d)
    gain = lambda g: g.reshape(1, -1).astype(F32)

    w_in = gla_w_in[0]
    c_v, c_g, c_r = d_qk, d_qk + d_v, d_qk + d_v + rank
    w_in_cat = jnp.concatenate(
        [w_in[:, :c_v], w_in[:, c_v:c_g], w_in[:, c_r:], jnp.pad(w_in[:, c_g:c_r], ((0, 0), (0, LANES - rank)))],
        axis=1).astype(BF16)
    qk, v, r, glr = _inproj(xf, gain(attn_norm[0]), w_in_cat, d_qk, d_v)
    wgu = jnp.pad(gla_w_gate_up[0], ((0, LANES - rank), (0, 0))).astype(BF16)
    o = _gla(qk, v, r, glr, wgu, gla_b_gate[0].reshape(1, hk), gain(gla_head_norm[0]), b_sz, s_len, heads, dk, dv)
    h, hn = _outproj(o, gla_w_out[0].astype(BF16), xf, gain(ffn_norm[0]))

    h = _ffn(hn, ffn_w_gate[0].astype(BF16), ffn_w_up[0].astype(BF16), ffn_w_down[0].astype(BF16), h)

    kv, q = _kvq(h, gain(kv_norm), gain(attn_norm[1]), sb_w_kv.astype(BF16), sb_w_q[0].astype(BF16))
    hd = sb_w_q.shape[2] // SB_HEADS
    o = _sb_attention(q, kv, b_sz, s_len, SB_HEADS, hd)

    wr = jnp.pad(moe_w_router[0], ((0, 0), (0, LANES - N_EXPERTS)))
    wr_hi = wr.astype(BF16)
    wr_lo = (wr - wr_hi.astype(F32)).astype(BF16)
    h, hn, info, counts = _route(o, sb_w_out[0].astype(BF16), h, gain(ffn_norm[1]), wr_hi, wr_lo)

    cnt = counts[0, :N_EXPERTS].astype(jnp.int32)
    padded = (cnt + MOE_TILE - 1) // MOE_TILE * MOE_TILE
    ends = jnp.cumsum(padded)
    offs = ends - padded
    e1, e2 = info[:, 0].astype(jnp.int32), info[:, 1].astype(jnp.int32)
    pos1 = offs[e1] + info[:, 4].astype(jnp.int32)
    pos2 = offs[e2] + info[:, 5].astype(jnp.int32)
    n_sorted = 2 * t + N_EXPERTS * MOE_TILE
    n_tiles = n_sorted // MOE_TILE
    tile_start = jnp.arange(n_tiles, dtype=jnp.int32) * MOE_TILE
    tile_expert = jnp.minimum(jnp.sum(tile_start[:, None] >= ends[None, :], axis=1), N_EXPERTS - 1).astype(jnp.int32)
    n_valid = (ends[-1] // MOE_TILE).astype(jnp.int32).reshape(1)

    xs = _scatter_rows(pos1, pos2, hn, n_sorted)
    y = _moe(tile_expert, n_valid, xs, moe_w_gate[0].astype(BF16), moe_w_up[0].astype(BF16),
             moe_w_down[0].astype(BF16))
    out = _combine(pos1, pos2, h, info, gain(final_norm), y)
    return out.reshape(b_sz, s_len, d)
```

```python
import functools

import jax
import jax.numpy as jnp
from jax import lax
from jax.experimental import pallas as pl
from jax.experimental.pallas import tpu as pltpu

F32 = jnp.float32
BF16 = jnp.bfloat16

RMS_EPS = 1e-6
LOG2_E = 1.4426950408889634
GLA_HEADS = 4
GLA_GATE_NORM = 16.0
GLA_CHUNK = 64
SB_HEADS = 8
N_EXPERTS = 8

LANES = 128
INFO_COLS = 8
V7X_VMEM_LIMIT_CAP = 60 * 1024 * 1024

TOKEN_TILE = 512
GLA_TILE = 256
GLA_BATCH = 2
SB_Q_TILE = 512
SB_K_TILE = 256
SB_HEAD_GROUP = 8
MOE_TILE = 512
MOE_F_TILE = 1792
MOE_F_CHUNK = 256
ROW_TILE = 256
ROW_UNROLL = 8


def _vmem_limit(estimate_bytes):
    return int(min(max(estimate_bytes * 5 // 4 + (4 << 20), 16 << 20), V7X_VMEM_LIMIT_CAP))


def _params(semantics, estimate_bytes):
    return pltpu.CompilerParams(dimension_semantics=semantics,
                                vmem_limit_bytes=_vmem_limit(estimate_bytes))


def _rms_scale(x):
    return lax.rsqrt(jnp.mean(x * x, axis=-1, keepdims=True) + RMS_EPS)


def _log_sigmoid(x):
    return jnp.minimum(x, 0.0) - jnp.log(1.0 + jnp.exp(-jnp.abs(x)))


def _silu(x):
    return x * (1.0 / (1.0 + jnp.exp(-x)))


def _dot(a, b):
    return jnp.dot(a, b, preferred_element_type=F32)


def _dot_nt(a, b):
    return lax.dot_general(a, b, (((1,), (1,)), ((), ())), preferred_element_type=F32)


def _dot_tn(a, b):
    return lax.dot_general(a, b, (((0,), (0,)), ((), ())), preferred_element_type=F32)


def _split_bf16(x):
    hi = x.astype(BF16)
    lo = (x - hi.astype(F32)).astype(BF16)
    return hi, lo


def _neg_abs(x):
    bits = lax.bitcast_convert_type(x, jnp.uint32) | jnp.uint32(0x80000000)
    return lax.bitcast_convert_type(bits, F32)


def _split_bf16_trunc(x):
    hi = lax.bitcast_convert_type(lax.bitcast_convert_type(x, jnp.uint32) & jnp.uint32(0xFFFF0000), F32)
    return hi.astype(BF16), (x - hi).astype(BF16)


def _resident(shape):
    return pl.BlockSpec(shape, lambda *_: (0,) * len(shape))


def _inproj_kernel(x_ref, g_ref, w_ref, qk_ref, v_ref, r_ref, glr_ref, *, d_qk, d_v):
    x = x_ref[...]
    xn = (x * _rms_scale(x) * g_ref[...]).astype(BF16)
    c0, c1, c2 = d_qk, d_qk + d_v, d_qk + 2 * d_v
    qk_ref[...] = _dot(xn, w_ref[:, 0:c0])
    v_ref[...] = _dot(xn, w_ref[:, c0:c1]).astype(BF16)
    r_ref[...] = _dot(xn, w_ref[:, c1:c2])
    glr_ref[...] = _dot(xn, w_ref[:, c2:c2 + LANES])


def _inproj(x, gain, w, d_qk, d_v):
    t, d = x.shape
    n = w.shape[1]
    tm = TOKEN_TILE
    est = 2 * (tm * d * 4 + d * n * 2 + tm * (d_qk * 4 + d_v * 2 + d_v * 4 + LANES * 4)) + 3 * tm * d * 4
    row = lambda i: (i, 0)
    return pl.pallas_call(
        functools.partial(_inproj_kernel, d_qk=d_qk, d_v=d_v),
        out_shape=(jax.ShapeDtypeStruct((t, d_qk), F32), jax.ShapeDtypeStruct((t, d_v), BF16),
                   jax.ShapeDtypeStruct((t, d_v), F32), jax.ShapeDtypeStruct((t, LANES), F32)),
        grid=(t // tm,),
        in_specs=[pl.BlockSpec((tm, d), row), _resident((1, d)), _resident((d, n))],
        out_specs=(pl.BlockSpec((tm, d_qk), row), pl.BlockSpec((tm, d_v), row),
                   pl.BlockSpec((tm, d_v), row), pl.BlockSpec((tm, LANES), row)),
        compiler_params=_params(("parallel",), est),
        name="gla_inproj",
    )(x, gain, w)


def _gla_kernel(qk_ref, v_ref, r_ref, glr_ref, wgu_ref, bg_ref, gh_ref, cum_ref, wo_ref, res_ref, gn_ref,
                h_ref, hn_ref, st_ref, og_ref, *, heads, dk, dv, chunk):
    nb, tc = qk_ref.shape[0], qk_ref.shape[1]
    hk = heads * dk

    @pl.when(pl.program_id(1) == 0)
    def _():
        st_ref[...] = jnp.zeros_like(st_ref)

    prepared = []
    for bb in range(nb):
        pre = _dot(glr_ref[bb].astype(BF16), wgu_ref[...]) + bg_ref[...]
        log_a = _log_sigmoid(pre) * (1.0 / GLA_GATE_NORM)
        hi, lo = _split_bf16(log_a)
        cums = _dot(cum_ref[...], jnp.concatenate([hi, lo], axis=0))
        b = cums[:tc]
        b_end = cums[tc:]
        qk = qk_ref[bb]
        q_dec = (qk[:, :hk] * (dk ** -0.5) * jnp.exp(b)).astype(BF16)
        k = qk[:, hk:]
        k_inv = (k * jnp.exp(-b)).astype(BF16)
        k_end = (k * jnp.exp(b_end - b)).astype(BF16)
        prepared.append((q_dec, k_inv, k_end, jnp.exp(b_end)))

    row = lax.broadcasted_iota(jnp.int32, (chunk, chunk), 0)
    col = lax.broadcasted_iota(jnp.int32, (chunk, chunk), 1)
    tri = col <= row
    gh = gh_ref[...]
    for c in range(tc // chunk):
        rows = slice(c * chunk, (c + 1) * chunk)
        for bb in range(nb):
            q_dec, k_inv, k_end, decay_end = prepared[bb]
            for h in range(heads):
                kc = slice(h * dk, (h + 1) * dk)
                vc = slice(h * dv, (h + 1) * dv)
                qd = q_dec[rows, kc]
                vv = v_ref[bb, rows, vc]
                st = st_ref[bb, h]
                attn = jnp.where(tri, _dot_nt(qd, k_inv[rows, kc]), 0.0).astype(BF16)
                o = _dot_nt(qd, st.astype(BF16)) + _dot(attn, vv)
                st_ref[bb, h] = decay_end[c * chunk:c * chunk + 1, kc] * st + _dot_tn(vv, k_end[rows, kc])
                o = o * _rms_scale(o) * gh
                og_ref[bb, rows, vc] = (o * _silu(r_ref[bb, rows, vc])).astype(BF16)

    for bb in range(nb):
        h = res_ref[bb] + _dot(og_ref[bb], wo_ref[...])
        h_ref[bb] = h
        hn_ref[bb] = (h * _rms_scale(h) * gn_ref[...]).astype(BF16)


def _gla_cum_matrix(tc, chunk):
    i = jnp.arange(tc)
    same = (i[:, None] // chunk) == (i[None, :] // chunk)
    lower = jnp.logical_and(same, i[None, :] <= i[:, None])
    m = jnp.concatenate([lower, same], axis=0).astype(BF16)
    return jnp.concatenate([m, m], axis=1)


def _gla(qk, v, r, glr, wgu, bgate, ghead, w_out, res, gain_next, b_sz, s_len, heads, dk, dv):
    tc = GLA_TILE
    hk, hv = heads * dk, heads * dv
    d = res.shape[-1]
    qk, v, r, glr, res = (a.reshape(b_sz, s_len, a.shape[-1]) for a in (qk, v, r, glr, res))
    cum = _gla_cum_matrix(tc, GLA_CHUNK)
    nb = GLA_BATCH if b_sz % GLA_BATCH == 0 else 1
    est = nb * (2 * tc * (2 * hk * 4 + hv * 2 + hv * 4 + LANES * 4 + 2 * d * 4 + d * 2) + heads * dv * dk * 4
                + tc * hv * 2 + 16 * tc * hk * 4) + 2 * hv * d * 2
    blk = lambda n: pl.BlockSpec((nb, tc, n), lambda b, s: (b, s, 0))
    h, hn = pl.pallas_call(
        functools.partial(_gla_kernel, heads=heads, dk=dk, dv=dv, chunk=GLA_CHUNK),
        out_shape=(jax.ShapeDtypeStruct((b_sz, s_len, d), F32), jax.ShapeDtypeStruct((b_sz, s_len, d), BF16)),
        grid=(b_sz // nb, s_len // tc),
        in_specs=[blk(2 * hk), blk(hv), blk(hv), blk(LANES), _resident(wgu.shape), _resident(bgate.shape),
                  _resident(ghead.shape), _resident(cum.shape), _resident(w_out.shape), blk(d),
                  _resident(gain_next.shape)],
        out_specs=(blk(d), blk(d)),
        scratch_shapes=[pltpu.VMEM((nb, heads, dv, dk), F32), pltpu.VMEM((nb, tc, hv), BF16)],
        compiler_params=_params(("parallel", "arbitrary"), est),
        name="gla_recurrence_outproj",
    )(qk, v, r, glr, wgu, bgate, ghead, cum, w_out, res, gain_next)
    return h.reshape(b_sz * s_len, d), hn.reshape(b_sz * s_len, d)


def _ffn_kernel(hn_ref, wg_ref, wu_ref, wd_ref, res_ref, out_ref, act_ref, *, f_chunk):
    hn = hn_ref[...]
    f_dim = wg_ref.shape[1]
    for c0 in range(0, f_dim, f_chunk):
        cols = slice(c0, min(c0 + f_chunk, f_dim))
        act_ref[:, cols] = (_silu(_dot(hn, wg_ref[:, cols])) * _dot(hn, wu_ref[:, cols])).astype(BF16)
    out_ref[...] = res_ref[...] + _dot(act_ref[...], wd_ref[...])


def _ffn(hn, wg, wu, wd, res):
    t, d = res.shape
    f = wg.shape[1]
    tm = TOKEN_TILE
    est = 2 * (tm * d * 2 + 3 * d * f * 2 + 2 * tm * d * 4) + tm * f * 2 + 6 * tm * 256 * 4
    row = lambda i: (i, 0)
    return pl.pallas_call(
        functools.partial(_ffn_kernel, f_chunk=256),
        out_shape=jax.ShapeDtypeStruct((t, d), F32),
        grid=(t // tm,),
        in_specs=[pl.BlockSpec((tm, d), row), _resident((d, f)), _resident((d, f)), _resident((f, d)),
                  pl.BlockSpec((tm, d), row)],
        out_specs=pl.BlockSpec((tm, d), row),
        scratch_shapes=[pltpu.VMEM((tm, f), BF16)],
        compiler_params=_params(("parallel",), est),
        name="dense_swiglu",
    )(hn, wg, wu, wd, res)


def _kvq_kernel(h_ref, gkv_ref, gq_ref, wkv_ref, wq_ref, kv_ref, q_ref):
    h = h_ref[...]
    y = h * _rms_scale(h)
    kv_ref[...] = _dot((y * gkv_ref[...]).astype(BF16), wkv_ref[...]).astype(BF16)
    q_ref[...] = _dot((y * gq_ref[...]).astype(BF16), wq_ref[...]).astype(BF16)


def _kvq(h, gkv, gq, wkv, wq):
    t, d = h.shape
    nkv, nq = wkv.shape[1], wq.shape[1]
    tm = TOKEN_TILE
    est = 2 * (tm * d * 4 + d * (nkv + nq) * 2 + tm * (nkv + nq) * 2) + 4 * tm * d * 4
    row = lambda i: (i, 0)
    return pl.pallas_call(
        _kvq_kernel,
        out_shape=(jax.ShapeDtypeStruct((t, nkv), BF16), jax.ShapeDtypeStruct((t, nq), BF16)),
        grid=(t // tm,),
        in_specs=[pl.BlockSpec((tm, d), row), _resident((1, d)), _resident((1, d)), _resident((d, nkv)),
                  _resident((d, nq))],
        out_specs=(pl.BlockSpec((tm, nkv), row), pl.BlockSpec((tm, nq), row)),
        compiler_params=_params(("parallel",), est),
        name="kv_q_proj",
    )(h, gkv, gq, wkv, wq)


def _sb_kernel(q_ref, k_ref, v_ref, cs_ref, o_ref, acc_ref, carry_ref, *, scale, group, hd, tk):
    tq = q_ref.shape[0]
    i = pl.program_id(2)
    cs = cs_ref[...]
    acc_ref[...] = jnp.zeros_like(acc_ref)
    carry_ref[...] = jnp.zeros_like(carry_ref)

    def block(jb, masked, r0=0):
        start = pl.multiple_of(jb * tk, tk)
        rows = slice(r0, tq)
        if masked:
            causal = (lax.broadcasted_iota(jnp.int32, (tq - r0, tk), 1) + jb * tk
                      < lax.broadcasted_iota(jnp.int32, (tq - r0, tk), 0) + (i * tq + r0))
        for g in range(group):
            hc = slice(g * hd, (g + 1) * hd)
            z = _dot_nt(q_ref[rows, hc], k_ref[pl.ds(start, tk), hc]) * (scale * LOG2_E)
            log_beta = jnp.minimum(z, 0.0) - jnp.log(1.0 + jnp.exp2(_neg_abs(z))) * LOG2_E
            log_keep = log_beta - z
            if masked:
                log_keep = jnp.where(causal, log_keep, 0.0)
            carry = carry_ref[g, rows]
            later = _dot(log_keep.astype(BF16), cs)
            w = jnp.exp2(log_beta + jnp.concatenate([carry] * (tk // LANES), axis=1) + later)
            if masked:
                w = jnp.where(causal, w, 0.0)
            carry_ref[g, rows] = carry + jnp.sum(log_keep, axis=-1, keepdims=True)
            acc_ref[rows, hc] += _dot(w.astype(BF16), v_ref[pl.ds(start, tk), hc])

    per = tq // tk
    for d in range(per):
        block((i + 1) * per - 1 - d, True, r0=(per - 1 - d) * tk)

    def body(n, c):
        block(i * per - 1 - n, False)
        return c

    lax.fori_loop(0, i * per, body, 0)
    o_ref[...] = acc_ref[...].astype(o_ref.dtype)


def _sb_cumsum_matrix(tk):
    j = jnp.arange(tk)
    return (j[:, None] > j[None, :]).astype(BF16)


def _sb_attention(q, kv, b_sz, s_len, heads, hd):
    tq, tk = SB_Q_TILE, SB_K_TILE
    q = q.reshape(b_sz, s_len, heads * hd)
    kv = kv.reshape(b_sz, s_len, 2 * heads * hd)
    cs = _sb_cumsum_matrix(tk)
    group = SB_HEAD_GROUP
    gw = group * hd
    n_groups = heads // group
    est = 2 * (tq * gw * 2 * 2 + 2 * s_len * gw * 2) + tq * gw * 4 + group * tq * LANES * 4 + group * 12 * tq * tk * 4
    out = pl.pallas_call(
        functools.partial(_sb_kernel, scale=hd ** -0.5, group=group, hd=hd, tk=tk),
        out_shape=jax.ShapeDtypeStruct((b_sz, s_len, heads * hd), BF16),
        grid=(b_sz, n_groups, s_len // tq),
        in_specs=[pl.BlockSpec((None, tq, gw), lambda b, h, i: (b, i, h)),
                  pl.BlockSpec((None, s_len, gw), lambda b, h, i: (b, 0, h)),
                  pl.BlockSpec((None, s_len, gw), lambda b, h, i: (b, 0, n_groups + h)),
                  _resident(cs.shape)],
        out_specs=pl.BlockSpec((None, tq, gw), lambda b, h, i: (b, i, h)),
        scratch_shapes=[pltpu.VMEM((tq, gw), F32), pltpu.VMEM((group, tq, LANES), F32)],
        compiler_params=_params(("parallel", "parallel", "arbitrary"), est),
        name="stick_breaking_attention",
    )(q, kv, kv, cs)
    return out.reshape(b_sz * s_len, heads * hd)


def _route_kernel(o_ref, w_ref, res_ref, g_ref, wr_hi_ref, wr_lo_ref, lt_ref,
                  h_ref, hn_ref, info_ref, cnt_ref, run_ref):
    @pl.when(pl.program_id(0) == 0)
    def _():
        run_ref[...] = jnp.zeros_like(run_ref)

    h = res_ref[...] + _dot(o_ref[...], w_ref[...])
    h_ref[...] = h
    hn = h * _rms_scale(h) * g_ref[...]
    hn_ref[...] = hn
    a_hi, a_lo = _split_bf16(hn)
    logits = _dot(a_hi, wr_hi_ref[...]) + (_dot(a_lo, wr_hi_ref[...]) + _dot(a_hi, wr_lo_ref[...]))
    lane = lax.broadcasted_iota(jnp.int32, logits.shape, 1).astype(F32)
    neg = jnp.float32(-jnp.inf)
    logits = jnp.where(lane < N_EXPERTS, logits, neg)
    v1 = jnp.max(logits, axis=-1, keepdims=True)
    i1 = jnp.min(jnp.where(logits == v1, lane, float(LANES)), axis=-1, keepdims=True)
    rest = jnp.where(lane == i1, neg, logits)
    v2 = jnp.max(rest, axis=-1, keepdims=True)
    i2 = jnp.min(jnp.where(rest == v2, lane, float(LANES)), axis=-1, keepdims=True)
    e = jnp.exp(v2 - v1)
    g1 = 1.0 / (1.0 + e)
    g2 = e / (1.0 + e)
    hot1 = lane == i1
    hot2 = lane == i2
    member = jnp.where(jnp.logical_or(hot1, hot2), 1.0, 0.0)
    before = _dot(lt_ref[...], member.astype(BF16)) + run_ref[0:1, :]
    rank1 = jnp.sum(jnp.where(hot1, before, 0.0), axis=-1, keepdims=True)
    rank2 = jnp.sum(jnp.where(hot2, before, 0.0), axis=-1, keepdims=True)
    total = run_ref[0:1, :] + jnp.sum(member, axis=0, keepdims=True)
    run_ref[...] = jnp.broadcast_to(total, run_ref.shape)
    cnt_ref[...] = jnp.broadcast_to(total, cnt_ref.shape)
    info = jnp.where(lane == 0, i1.astype(F32), 0.0)
    info = jnp.where(lane == 1, i2.astype(F32), info)
    info = jnp.where(lane == 2, g1, info)
    info = jnp.where(lane == 3, g2, info)
    info = jnp.where(lane == 4, rank1, info)
    info = jnp.where(lane == 5, rank2, info)
    info_ref[...] = info[:, :INFO_COLS]


def _route(o, w, res, gain, wr_hi, wr_lo):
    t, d = res.shape
    k = o.shape[1]
    tm = TOKEN_TILE
    lt = jnp.tril(jnp.ones((tm, tm), BF16), -1)
    est = 2 * (tm * k * 2 + k * d * 2 + 3 * tm * d * 4 + 2 * d * LANES * 2 + tm * tm * 2 + tm * LANES * 4) + 6 * tm * d * 4
    row = lambda i: (i, 0)
    return pl.pallas_call(
        _route_kernel,
        out_shape=(jax.ShapeDtypeStruct((t, d), F32), jax.ShapeDtypeStruct((t, d), F32),
                   jax.ShapeDtypeStruct((t, INFO_COLS), F32), jax.ShapeDtypeStruct((8, LANES), F32)),
        grid=(t // tm,),
        in_specs=[pl.BlockSpec((tm, k), row), _resident((k, d)), pl.BlockSpec((tm, d), row), _resident((1, d)),
                  _resident((d, LANES)), _resident((d, LANES)), _resident((tm, tm))],
        out_specs=(pl.BlockSpec((tm, d), row), pl.BlockSpec((tm, d), row), pl.BlockSpec((tm, INFO_COLS), row),
                   _resident((8, LANES))),
        scratch_shapes=[pltpu.VMEM((8, LANES), F32)],
        compiler_params=_params(("arbitrary",), est),
        name="outproj_norm_router",
    )(o, w, res, gain, wr_hi, wr_lo, lt)


def _scatter_kernel(gap_lo_ref, gap_hi_ref, pos1_ref, pos2_ref, src_ref, dst_ref, zero_ref, sem, zero_sem):
    tile = pos1_ref.shape[0]

    @pl.when(pl.program_id(0) == 0)
    def _():
        zero_ref[...] = jnp.zeros_like(zero_ref)

        def zero_copy(r):
            return pltpu.make_async_copy(zero_ref.at[pl.ds(0, 1)], dst_ref.at[pl.ds(r, 1)], zero_sem)

        def zero_start(r, c):
            zero_copy(r).start()
            return c

        def zero_wait(r, c):
            zero_copy(r).wait()
            return c

        for g in range(gap_lo_ref.shape[0]):
            lax.fori_loop(gap_lo_ref[g], gap_hi_ref[g], zero_start, 0)
        for g in range(gap_lo_ref.shape[0]):
            lax.fori_loop(gap_lo_ref[g], gap_hi_ref[g], zero_wait, 0)

    def copies(r):
        src = src_ref.at[pl.ds(r, 1)]
        return (pltpu.make_async_copy(src, dst_ref.at[pl.ds(pos1_ref[r], 1)], sem),
                pltpu.make_async_copy(src, dst_ref.at[pl.ds(pos2_ref[r], 1)], sem))

    def start(r, c):
        for cp in copies(r):
            cp.start()
        return c

    def wait(r, c):
        for cp in copies(r):
            cp.wait()
        return c

    lax.fori_loop(0, tile, start, 0, unroll=ROW_UNROLL)
    lax.fori_loop(0, tile, wait, 0, unroll=ROW_UNROLL)


def _scatter_rows(gap_lo, gap_hi, pos1, pos2, src, n_sorted):
    t, d = src.shape
    tile = ROW_TILE
    smem = lambda: pl.BlockSpec((tile,), lambda i, lo, hi: (i,), memory_space=pltpu.SMEM)
    grid_spec = pltpu.PrefetchScalarGridSpec(
        num_scalar_prefetch=2,
        grid=(t // tile,),
        in_specs=[smem(), smem(), pl.BlockSpec((tile, d), lambda i, lo, hi: (i, 0))],
        out_specs=pl.BlockSpec(memory_space=pl.ANY),
        scratch_shapes=[pltpu.VMEM((8, d), src.dtype), pltpu.SemaphoreType.DMA(()), pltpu.SemaphoreType.DMA(())])
    return pl.pallas_call(
        _scatter_kernel,
        out_shape=jax.ShapeDtypeStruct((n_sorted, d), src.dtype),
        grid_spec=grid_spec,
        compiler_params=pltpu.CompilerParams(dimension_semantics=("arbitrary",),
                                             vmem_limit_bytes=_vmem_limit(2 * tile * d * 4)),
        name="moe_row_scatter",
    )(gap_lo, gap_hi, pos1, pos2, src)


def _moe_kernel(te_ref, nv_ref, x_ref, wg_ref, wu_ref, wd_ref, y_ref, xb_ref, act_ref, acc_ref, *, f_chunk):
    del te_ref
    j = pl.program_id(0)
    f = pl.program_id(1)
    nf = pl.num_programs(1)
    tf = wg_ref.shape[1]

    @pl.when(j < nv_ref[0])
    def _():
        @pl.when(f == 0)
        def _():
            xb_ref[...] = x_ref[...].astype(BF16)

        xb = xb_ref[...]
        for c0 in range(0, tf, f_chunk):
            cols = slice(c0, c0 + f_chunk)
            act_ref[:, cols] = (_silu(_dot(xb, wg_ref[:, cols])) * _dot(xb, wu_ref[:, cols])).astype(BF16)
        part = _dot(act_ref[...], wd_ref[...])

        @pl.when(jnp.logical_and(f == 0, nf > 1))
        def _():
            acc_ref[...] = part

        @pl.when(jnp.logical_and(f > 0, f < nf - 1))
        def _():
            acc_ref[...] += part

        @pl.when(jnp.logical_and(f == nf - 1, nf > 1))
        def _():
            y_ref[...] = acc_ref[...] + part

        @pl.when(nf == 1)
        def _():
            y_ref[...] = part

    @pl.when(jnp.logical_and(j >= nv_ref[0], f == nf - 1))
    def _():
        y_ref[...] = jnp.zeros_like(y_ref)


def _moe_f_tile(f_dim):
    nf = 1
    while f_dim % nf or f_dim // nf > MOE_F_TILE or (f_dim // nf) % MOE_F_CHUNK:
        nf += 1
    return f_dim // nf


def _moe(tile_expert, n_valid, xs, wg, wu, wd):
    p, d = xs.shape
    f_dim = wg.shape[2]
    tm, tf = MOE_TILE, _moe_f_tile(f_dim)
    nf = f_dim // tf

    def tile_idx(j, nv):
        return jnp.minimum(j, nv[0] - 1)

    def f_idx(j, f, nv):
        return jnp.where(j < nv[0], f, nf - 1)

    grid_spec = pltpu.PrefetchScalarGridSpec(
        num_scalar_prefetch=2,
        grid=(p // tm, nf),
        in_specs=[pl.BlockSpec((tm, d), lambda j, f, te, nv: (tile_idx(j, nv), 0)),
                  pl.BlockSpec((None, d, tf), lambda j, f, te, nv: (te[tile_idx(j, nv)], 0, f_idx(j, f, nv))),
                  pl.BlockSpec((None, d, tf), lambda j, f, te, nv: (te[tile_idx(j, nv)], 0, f_idx(j, f, nv))),
                  pl.BlockSpec((None, tf, d), lambda j, f, te, nv: (te[tile_idx(j, nv)], f_idx(j, f, nv), 0))],
        out_specs=pl.BlockSpec((tm, d), lambda j, f, te, nv: (j, 0)),
        scratch_shapes=[pltpu.VMEM((tm, d), BF16), pltpu.VMEM((tm, tf), BF16), pltpu.VMEM((tm, d), F32)])
    est = 2 * (2 * tm * d * 4 + 3 * d * tf * 2) + tm * d * 6 + tm * tf * 2 + 6 * tm * MOE_F_CHUNK * 4 + tm * d * 4
    return pl.pallas_call(
        functools.partial(_moe_kernel, f_chunk=MOE_F_CHUNK),
        out_shape=jax.ShapeDtypeStruct((p, d), F32),
        grid_spec=grid_spec,
        compiler_params=_params(("arbitrary", "arbitrary"), est),
        name="moe_grouped_swiglu",
    )(tile_expert, n_valid, xs, wg, wu, wd)


def _combine_kernel(pos1_ref, pos2_ref, nxt1_ref, nxt2_ref, h_ref, info_ref, g_ref, y_ref, out_ref, buf_ref, sem):
    tile = h_ref.shape[0]
    i = pl.program_id(0)
    slot = lax.rem(i, 2)

    def copies(p1_ref, p2_ref, s, r):
        return (pltpu.make_async_copy(y_ref.at[pl.ds(p1_ref[r], 1)], buf_ref.at[s, 0, pl.ds(r, 1)], sem.at[s]),
                pltpu.make_async_copy(y_ref.at[pl.ds(p2_ref[r], 1)], buf_ref.at[s, 1, pl.ds(r, 1)], sem.at[s]))

    def start_all(p1_ref, p2_ref, s):
        def start(r, c):
            for cp in copies(p1_ref, p2_ref, s, r):
                cp.start()
            return c
        lax.fori_loop(0, tile, start, 0, unroll=ROW_UNROLL)

    @pl.when(i == 0)
    def _():
        start_all(pos1_ref, pos2_ref, slot)

    @pl.when(i + 1 < pl.num_programs(0))
    def _():
        start_all(nxt1_ref, nxt2_ref, 1 - slot)

    def wait(r, c):
        for cp in copies(pos1_ref, pos2_ref, slot, r):
            cp.wait()
        return c

    lax.fori_loop(0, tile, wait, 0, unroll=ROW_UNROLL)
    info = info_ref[...]
    h = h_ref[...] + (info[:, 2:3] * buf_ref[slot, 0] + info[:, 3:4] * buf_ref[slot, 1])
    out_ref[...] = h * _rms_scale(h) * g_ref[...]


def _combine(pos1, pos2, h, info, gain, y):
    t, d = h.shape
    tile = ROW_TILE
    n = t // tile
    smem = lambda step: pl.BlockSpec((tile,), lambda i: (jnp.minimum(i + step, n - 1),), memory_space=pltpu.SMEM)
    row = lambda i: (i, 0)
    est = 2 * (2 * tile * d * 4 + tile * LANES * 4) + 4 * tile * d * 4 + 4 * tile * d * 4
    return pl.pallas_call(
        _combine_kernel,
        out_shape=jax.ShapeDtypeStruct((t, d), F32),
        grid=(n,),
        in_specs=[smem(0), smem(0), smem(1), smem(1), pl.BlockSpec((tile, d), row), pl.BlockSpec((tile, INFO_COLS), row),
                  _resident((1, d)), pl.BlockSpec(memory_space=pl.ANY)],
        out_specs=pl.BlockSpec((tile, d), row),
        scratch_shapes=[pltpu.VMEM((2, 2, tile, d), F32), pltpu.SemaphoreType.DMA((2,))],
        compiler_params=_params(("arbitrary",), est),
        name="moe_combine_final_norm",
    )(pos1, pos2, pos1, pos2, h, info, gain, y)


def kernel(x, attn_norm, ffn_norm, kv_norm, final_norm, gla_w_in, gla_w_gate_up, gla_b_gate, gla_head_norm,
           gla_w_out, sb_w_kv, sb_w_q, sb_w_out, ffn_w_gate, ffn_w_up, ffn_w_down, moe_w_router, moe_w_gate,
           moe_w_up, moe_w_down):
    b_sz, s_len, d = x.shape
    t = b_sz * s_len
    heads = GLA_HEADS
    d_qk = gla_w_gate_up.shape[2] * 2
    hk = d_qk // 2
    rank = gla_w_gate_up.shape[1]
    d_v = gla_w_out.shape[1]
    dk, dv = hk // heads, d_v // heads
    assert gla_w_in.shape[2] == d_qk + 2 * d_v + rank and rank <= LANES
    assert t % TOKEN_TILE == 0 and s_len % GLA_TILE == 0 and s_len % SB_Q_TILE == 0
    assert attn_norm.shape[0] == 2 and moe_w_gate.shape[1] == N_EXPERTS

    xf = x.reshape(t, d)
    gain = lambda g: g.reshape(1, -1).astype(F32)

    w_in = gla_w_in[0]
    c_v, c_g, c_r = d_qk, d_qk + d_v, d_qk + d_v + rank
    w_in_cat = jnp.concatenate(
        [w_in[:, :c_v], w_in[:, c_v:c_g], w_in[:, c_r:], jnp.pad(w_in[:, c_g:c_r], ((0, 0), (0, LANES - rank)))],
        axis=1).astype(BF16)
    qk, v, r, glr = _inproj(xf, gain(attn_norm[0]), w_in_cat, d_qk, d_v)
    wgu = jnp.pad(gla_w_gate_up[0], ((0, LANES - rank), (0, 0))).astype(BF16)
    h, hn = _gla(qk, v, r, glr, wgu, gla_b_gate[0].reshape(1, hk), gain(gla_head_norm[0]),
                 gla_w_out[0].astype(BF16), xf, gain(ffn_norm[0]), b_sz, s_len, heads, dk, dv)

    h = _ffn(hn, ffn_w_gate[0].astype(BF16), ffn_w_up[0].astype(BF16), ffn_w_down[0].astype(BF16), h)

    kv, q = _kvq(h, gain(kv_norm), gain(attn_norm[1]), sb_w_kv.astype(BF16), sb_w_q[0].astype(BF16))
    hd = sb_w_q.shape[2] // SB_HEADS
    o = _sb_attention(q, kv, b_sz, s_len, SB_HEADS, hd)

    wr = jnp.pad(moe_w_router[0], ((0, 0), (0, LANES - N_EXPERTS)))
    wr_hi = wr.astype(BF16)
    wr_lo = (wr - wr_hi.astype(F32)).astype(BF16)
    h, hn, info, counts = _route(o, sb_w_out[0].astype(BF16), h, gain(ffn_norm[1]), wr_hi, wr_lo)

    cnt = counts[0, :N_EXPERTS].astype(jnp.int32)
    padded = (cnt + MOE_TILE - 1) // MOE_TILE * MOE_TILE
    ends = jnp.cumsum(padded)
    offs = ends - padded
    e1, e2 = info[:, 0].astype(jnp.int32), info[:, 1].astype(jnp.int32)
    pos1 = offs[e1] + info[:, 4].astype(jnp.int32)
    pos2 = offs[e2] + info[:, 5].astype(jnp.int32)
    n_sorted = 2 * t + N_EXPERTS * MOE_TILE
    n_tiles = n_sorted // MOE_TILE
    tile_start = jnp.arange(n_tiles, dtype=jnp.int32) * MOE_TILE
    tile_expert = jnp.minimum(jnp.sum(tile_start[:, None] >= ends[None, :], axis=1), N_EXPERTS - 1).astype(jnp.int32)
    n_valid = (ends[-1] // MOE_TILE).astype(jnp.int32).reshape(1)

    gap_lo = jnp.concatenate([offs + cnt, ends[-1:]]).astype(jnp.int32)
    gap_hi = jnp.concatenate([ends, jnp.full((1,), n_sorted, ends.dtype)]).astype(jnp.int32)
    xs = _scatter_rows(gap_lo, gap_hi, pos1, pos2, hn, n_sorted)
    y = _moe(tile_expert, n_valid, xs, moe_w_gate[0].astype(BF16), moe_w_up[0].astype(BF16),
             moe_w_down[0].astype(BF16))
    out = _combine(pos1, pos2, h, info, gain(final_norm), y)
    return out.reshape(b_sz, s_len, d)
```

```python
import functools

import jax
import jax.numpy as jnp
from jax import lax
from jax.experimental import pallas as pl
from jax.experimental.pallas import tpu as pltpu

F32 = jnp.float32
BF16 = jnp.bfloat16

RMS_EPS = 1e-6
LOG2_E = 1.4426950408889634
GLA_HEADS = 4
GLA_GATE_NORM = 16.0
GLA_CHUNK = 64
SB_HEADS = 8
N_EXPERTS = 8

LANES = 128
INFO_COLS = 8
V7X_VMEM_LIMIT_CAP = 60 * 1024 * 1024

TOKEN_TILE = 512
GLA_TILE = 256
GLA_BATCH = 2
SB_Q_TILE = 1024
SB_K_TILE = 256
SB_HEAD_GROUP = 8
MOE_TILE = 512
MOE_F_TILE = 3584
MOE_F_CHUNK = 256
ROW_TILE = 256
ROW_UNROLL = 8


def _vmem_limit(estimate_bytes):
    return int(min(max(estimate_bytes * 5 // 4 + (4 << 20), 16 << 20), V7X_VMEM_LIMIT_CAP))


def _params(semantics, estimate_bytes):
    return pltpu.CompilerParams(dimension_semantics=semantics,
                                vmem_limit_bytes=_vmem_limit(estimate_bytes))


def _rms_scale(x):
    return lax.rsqrt(jnp.mean(x * x, axis=-1, keepdims=True) + RMS_EPS)


def _log_sigmoid(x):
    return jnp.minimum(x, 0.0) - jnp.log(1.0 + jnp.exp(-jnp.abs(x)))


def _silu(x):
    return x * (1.0 / (1.0 + jnp.exp(-x)))


def _dot(a, b):
    return jnp.dot(a, b, preferred_element_type=F32)


def _dot_nt(a, b):
    return lax.dot_general(a, b, (((1,), (1,)), ((), ())), preferred_element_type=F32)


def _dot_tn(a, b):
    return lax.dot_general(a, b, (((0,), (0,)), ((), ())), preferred_element_type=F32)


def _split_bf16(x):
    hi = x.astype(BF16)
    lo = (x - hi.astype(F32)).astype(BF16)
    return hi, lo


def _neg_abs(x):
    bits = lax.bitcast_convert_type(x, jnp.uint32) | jnp.uint32(0x80000000)
    return lax.bitcast_convert_type(bits, F32)


def _split_bf16_trunc(x):
    hi = lax.bitcast_convert_type(lax.bitcast_convert_type(x, jnp.uint32) & jnp.uint32(0xFFFF0000), F32)
    return hi.astype(BF16), (x - hi).astype(BF16)


def _resident(shape):
    return pl.BlockSpec(shape, lambda *_: (0,) * len(shape))


def _inproj_kernel(x_ref, g_ref, w_ref, qk_ref, v_ref, r_ref, glr_ref, *, d_qk, d_v):
    x = x_ref[...]
    xn = (x * _rms_scale(x) * g_ref[...]).astype(BF16)
    c0, c1, c2 = d_qk, d_qk + d_v, d_qk + 2 * d_v
    qk_ref[...] = _dot(xn, w_ref[:, 0:c0])
    v_ref[...] = _dot(xn, w_ref[:, c0:c1]).astype(BF16)
    r_ref[...] = _dot(xn, w_ref[:, c1:c2])
    glr_ref[...] = _dot(xn, w_ref[:, c2:c2 + LANES])


def _inproj(x, gain, w, d_qk, d_v):
    t, d = x.shape
    n = w.shape[1]
    tm = TOKEN_TILE
    est = 2 * (tm * d * 4 + d * n * 2 + tm * (d_qk * 4 + d_v * 2 + d_v * 4 + LANES * 4)) + 3 * tm * d * 4
    row = lambda i: (i, 0)
    return pl.pallas_call(
        functools.partial(_inproj_kernel, d_qk=d_qk, d_v=d_v),
        out_shape=(jax.ShapeDtypeStruct((t, d_qk), F32), jax.ShapeDtypeStruct((t, d_v), BF16),
                   jax.ShapeDtypeStruct((t, d_v), F32), jax.ShapeDtypeStruct((t, LANES), F32)),
        grid=(t // tm,),
        in_specs=[pl.BlockSpec((tm, d), row), _resident((1, d)), _resident((d, n))],
        out_specs=(pl.BlockSpec((tm, d_qk), row), pl.BlockSpec((tm, d_v), row),
                   pl.BlockSpec((tm, d_v), row), pl.BlockSpec((tm, LANES), row)),
        compiler_params=_params(("parallel",), est),
        name="gla_inproj",
    )(x, gain, w)


def _gla_kernel(qk_ref, v_ref, r_ref, glr_ref, wgu_ref, bg_ref, gh_ref, cum_ref, wo_ref, res_ref, gn_ref,
                h_ref, hn_ref, st_ref, og_ref, *, heads, dk, dv, chunk):
    nb, tc = qk_ref.shape[0], qk_ref.shape[1]
    hk = heads * dk

    @pl.when(pl.program_id(1) == 0)
    def _():
        st_ref[...] = jnp.zeros_like(st_ref)

    prepared = []
    for bb in range(nb):
        pre = _dot(glr_ref[bb].astype(BF16), wgu_ref[...]) + bg_ref[...]
        log_a = _log_sigmoid(pre) * (1.0 / GLA_GATE_NORM)
        hi, lo = _split_bf16(log_a)
        cums = _dot(cum_ref[...], jnp.concatenate([hi, lo], axis=0))
        b = cums[:tc]
        b_end = cums[tc:]
        qk = qk_ref[bb]
        q_dec = (qk[:, :hk] * (dk ** -0.5) * jnp.exp(b)).astype(BF16)
        k = qk[:, hk:]
        k_inv = (k * jnp.exp(-b)).astype(BF16)
        k_end = (k * jnp.exp(b_end - b)).astype(BF16)
        prepared.append((q_dec, k_inv, k_end, jnp.exp(b_end)))

    row = lax.broadcasted_iota(jnp.int32, (chunk, chunk), 0)
    col = lax.broadcasted_iota(jnp.int32, (chunk, chunk), 1)
    tri = col <= row
    gh = gh_ref[...]
    for c in range(tc // chunk):
        rows = slice(c * chunk, (c + 1) * chunk)
        for bb in range(nb):
            q_dec, k_inv, k_end, decay_end = prepared[bb]
            for h in range(heads):
                kc = slice(h * dk, (h + 1) * dk)
                vc = slice(h * dv, (h + 1) * dv)
                qd = q_dec[rows, kc]
                vv = v_ref[bb, rows, vc]
                st = st_ref[bb, h]
                attn = jnp.where(tri, _dot_nt(qd, k_inv[rows, kc]), 0.0).astype(BF16)
                o = _dot_nt(qd, st.astype(BF16)) + _dot(attn, vv)
                st_ref[bb, h] = decay_end[c * chunk:c * chunk + 1, kc] * st + _dot_tn(vv, k_end[rows, kc])
                o = o * _rms_scale(o) * gh
                og_ref[bb, rows, vc] = (o * _silu(r_ref[bb, rows, vc])).astype(BF16)

    for bb in range(nb):
        h = res_ref[bb] + _dot(og_ref[bb], wo_ref[...])
        h_ref[bb] = h
        hn_ref[bb] = (h * _rms_scale(h) * gn_ref[...]).astype(BF16)


def _gla_cum_matrix(tc, chunk):
    i = jnp.arange(tc)
    same = (i[:, None] // chunk) == (i[None, :] // chunk)
    lower = jnp.logical_and(same, i[None, :] <= i[:, None])
    m = jnp.concatenate([lower, same], axis=0).astype(BF16)
    return jnp.concatenate([m, m], axis=1)


def _gla(qk, v, r, glr, wgu, bgate, ghead, w_out, res, gain_next, b_sz, s_len, heads, dk, dv):
    tc = GLA_TILE
    hk, hv = heads * dk, heads * dv
    d = res.shape[-1]
    qk, v, r, glr, res = (a.reshape(b_sz, s_len, a.shape[-1]) for a in (qk, v, r, glr, res))
    cum = _gla_cum_matrix(tc, GLA_CHUNK)
    nb = GLA_BATCH if b_sz % GLA_BATCH == 0 else 1
    est = nb * (2 * tc * (2 * hk * 4 + hv * 2 + hv * 4 + LANES * 4 + 2 * d * 4 + d * 2) + heads * dv * dk * 4
                + tc * hv * 2 + 16 * tc * hk * 4) + 2 * hv * d * 2
    blk = lambda n: pl.BlockSpec((nb, tc, n), lambda b, s: (b, s, 0))
    h, hn = pl.pallas_call(
        functools.partial(_gla_kernel, heads=heads, dk=dk, dv=dv, chunk=GLA_CHUNK),
        out_shape=(jax.ShapeDtypeStruct((b_sz, s_len, d), F32), jax.ShapeDtypeStruct((b_sz, s_len, d), BF16)),
        grid=(b_sz // nb, s_len // tc),
        in_specs=[blk(2 * hk), blk(hv), blk(hv), blk(LANES), _resident(wgu.shape), _resident(bgate.shape),
                  _resident(ghead.shape), _resident(cum.shape), _resident(w_out.shape), blk(d),
                  _resident(gain_next.shape)],
        out_specs=(blk(d), blk(d)),
        scratch_shapes=[pltpu.VMEM((nb, heads, dv, dk), F32), pltpu.VMEM((nb, tc, hv), BF16)],
        compiler_params=_params(("parallel", "arbitrary"), est),
        name="gla_recurrence_outproj",
    )(qk, v, r, glr, wgu, bgate, ghead, cum, w_out, res, gain_next)
    return h.reshape(b_sz * s_len, d), hn.reshape(b_sz * s_len, d)


def _ffn_kernel(hn_ref, wg_ref, wu_ref, wd_ref, res_ref, out_ref, act_ref, *, f_chunk):
    hn = hn_ref[...]
    f_dim = wg_ref.shape[1]
    for c0 in range(0, f_dim, f_chunk):
        cols = slice(c0, min(c0 + f_chunk, f_dim))
        act_ref[:, cols] = (_silu(_dot(hn, wg_ref[:, cols])) * _dot(hn, wu_ref[:, cols])).astype(BF16)
    out_ref[...] = res_ref[...] + _dot(act_ref[...], wd_ref[...])


def _ffn(hn, wg, wu, wd, res):
    t, d = res.shape
    f = wg.shape[1]
    tm = TOKEN_TILE
    est = 2 * (tm * d * 2 + 3 * d * f * 2 + 2 * tm * d * 4) + tm * f * 2 + 6 * tm * 256 * 4
    row = lambda i: (i, 0)
    return pl.pallas_call(
        functools.partial(_ffn_kernel, f_chunk=256),
        out_shape=jax.ShapeDtypeStruct((t, d), F32),
        grid=(t // tm,),
        in_specs=[pl.BlockSpec((tm, d), row), _resident((d, f)), _resident((d, f)), _resident((f, d)),
                  pl.BlockSpec((tm, d), row)],
        out_specs=pl.BlockSpec((tm, d), row),
        scratch_shapes=[pltpu.VMEM((tm, f), BF16)],
        compiler_params=_params(("parallel",), est),
        name="dense_swiglu",
    )(hn, wg, wu, wd, res)


def _kvq_kernel(h_ref, gkv_ref, gq_ref, wkv_ref, wq_ref, kv_ref, q_ref, *, q_scale):
    h = h_ref[...]
    y = h * _rms_scale(h)
    kv_ref[...] = _dot((y * gkv_ref[...]).astype(BF16), wkv_ref[...]).astype(BF16)
    q_ref[...] = (_dot((y * gq_ref[...]).astype(BF16), wq_ref[...]) * q_scale).astype(BF16)


def _kvq(h, gkv, gq, wkv, wq, q_scale):
    t, d = h.shape
    nkv, nq = wkv.shape[1], wq.shape[1]
    tm = TOKEN_TILE
    est = 2 * (tm * d * 4 + d * (nkv + nq) * 2 + tm * (nkv + nq) * 2) + 4 * tm * d * 4
    row = lambda i: (i, 0)
    return pl.pallas_call(
        functools.partial(_kvq_kernel, q_scale=q_scale),
        out_shape=(jax.ShapeDtypeStruct((t, nkv), BF16), jax.ShapeDtypeStruct((t, nq), BF16)),
        grid=(t // tm,),
        in_specs=[pl.BlockSpec((tm, d), row), _resident((1, d)), _resident((1, d)), _resident((d, nkv)),
                  _resident((d, nq))],
        out_specs=(pl.BlockSpec((tm, nkv), row), pl.BlockSpec((tm, nq), row)),
        compiler_params=_params(("parallel",), est),
        name="kv_q_proj",
    )(h, gkv, gq, wkv, wq)


def _sb_kernel(q_ref, k_ref, v_ref, cs_ref, o_ref, acc_ref, carry_ref, *, group, hd, tk):
    tq = q_ref.shape[0]
    i = pl.program_id(2)
    cs = cs_ref[...]
    acc_ref[...] = jnp.zeros_like(acc_ref)
    carry_ref[...] = jnp.zeros_like(carry_ref)

    def block(jb, masked, r0=0):
        start = pl.multiple_of(jb * tk, tk)
        rows = slice(r0, tq)
        if masked:
            causal = (lax.broadcasted_iota(jnp.int32, (tq - r0, tk), 1) + jb * tk
                      < lax.broadcasted_iota(jnp.int32, (tq - r0, tk), 0) + (i * tq + r0))
        for g in range(group):
            hc = slice(g * hd, (g + 1) * hd)
            z = _dot_nt(q_ref[rows, hc], k_ref[pl.ds(start, tk), hc])
            log_beta = jnp.minimum(z, 0.0) - jnp.log(1.0 + jnp.exp2(_neg_abs(z))) * LOG2_E
            log_keep = log_beta - z
            if masked:
                log_keep = jnp.where(causal, log_keep, 0.0)
            carry = carry_ref[g, rows]
            later = _dot(log_keep.astype(BF16), cs)
            w = jnp.exp2(log_beta + jnp.concatenate([carry] * (tk // LANES), axis=1) + later)
            if masked:
                w = jnp.where(causal, w, 0.0)
            carry_ref[g, rows] = carry + jnp.sum(log_keep, axis=-1, keepdims=True)
            acc_ref[rows, hc] += _dot(w.astype(BF16), v_ref[pl.ds(start, tk), hc])

    per = tq // tk
    for d in range(per):
        block((i + 1) * per - 1 - d, True, r0=(per - 1 - d) * tk)

    def body(n, c):
        block(i * per - 1 - n, False)
        return c

    lax.fori_loop(0, i * per, body, 0)
    o_ref[...] = acc_ref[...].astype(o_ref.dtype)


def _sb_cumsum_matrix(tk):
    j = jnp.arange(tk)
    return (j[:, None] > j[None, :]).astype(BF16)


def _sb_attention(q, kv, b_sz, s_len, heads, hd):
    tq, tk = SB_Q_TILE, SB_K_TILE
    q = q.reshape(b_sz, s_len, heads * hd)
    kv = kv.reshape(b_sz, s_len, 2 * heads * hd)
    cs = _sb_cumsum_matrix(tk)
    group = SB_HEAD_GROUP
    gw = group * hd
    n_groups = heads // group
    est = 2 * (tq * gw * 2 * 2 + 2 * s_len * gw * 2) + tq * gw * 4 + group * tq * LANES * 4 + group * 12 * tq * tk * 4
    out = pl.pallas_call(
        functools.partial(_sb_kernel, group=group, hd=hd, tk=tk),
        out_shape=jax.ShapeDtypeStruct((b_sz, s_len, heads * hd), BF16),
        grid=(b_sz, n_groups, s_len // tq),
        in_specs=[pl.BlockSpec((None, tq, gw), lambda b, h, i: (b, i, h)),
                  pl.BlockSpec((None, s_len, gw), lambda b, h, i: (b, 0, h)),
                  pl.BlockSpec((None, s_len, gw), lambda b, h, i: (b, 0, n_groups + h)),
                  _resident(cs.shape)],
        out_specs=pl.BlockSpec((None, tq, gw), lambda b, h, i: (b, i, h)),
        scratch_shapes=[pltpu.VMEM((tq, gw), F32), pltpu.VMEM((group, tq, LANES), F32)],
        compiler_params=_params(("parallel", "parallel", "arbitrary"), est),
        name="stick_breaking_attention",
    )(q, kv, kv, cs)
    return out.reshape(b_sz * s_len, heads * hd)


def _route_kernel(o_ref, w_ref, res_ref, g_ref, wr_hi_ref, wr_lo_ref, lt_ref,
                  h_ref, hn_ref, info_ref, cnt_ref, run_ref):
    @pl.when(pl.program_id(0) == 0)
    def _():
        run_ref[...] = jnp.zeros_like(run_ref)

    h = res_ref[...] + _dot(o_ref[...], w_ref[...])
    h_ref[...] = h
    hn = h * _rms_scale(h) * g_ref[...]
    hn_ref[...] = hn
    a_hi, a_lo = _split_bf16(hn)
    logits = _dot(a_hi, wr_hi_ref[...]) + (_dot(a_lo, wr_hi_ref[...]) + _dot(a_hi, wr_lo_ref[...]))
    lane = lax.broadcasted_iota(jnp.int32, logits.shape, 1).astype(F32)
    neg = jnp.float32(-jnp.inf)
    logits = jnp.where(lane < N_EXPERTS, logits, neg)
    v1 = jnp.max(logits, axis=-1, keepdims=True)
    i1 = jnp.min(jnp.where(logits == v1, lane, float(LANES)), axis=-1, keepdims=True)
    rest = jnp.where(lane == i1, neg, logits)
    v2 = jnp.max(rest, axis=-1, keepdims=True)
    i2 = jnp.min(jnp.where(rest == v2, lane, float(LANES)), axis=-1, keepdims=True)
    e = jnp.exp(v2 - v1)
    g1 = 1.0 / (1.0 + e)
    g2 = e / (1.0 + e)
    hot1 = lane == i1
    hot2 = lane == i2
    member = jnp.where(jnp.logical_or(hot1, hot2), 1.0, 0.0)
    before = _dot(lt_ref[...], member.astype(BF16)) + run_ref[0:1, :]
    rank1 = jnp.sum(jnp.where(hot1, before, 0.0), axis=-1, keepdims=True)
    rank2 = jnp.sum(jnp.where(hot2, before, 0.0), axis=-1, keepdims=True)
    total = run_ref[0:1, :] + jnp.sum(member, axis=0, keepdims=True)
    run_ref[...] = jnp.broadcast_to(total, run_ref.shape)
    cnt_ref[...] = jnp.broadcast_to(total, cnt_ref.shape)
    info = jnp.where(lane == 0, i1.astype(F32), 0.0)
    info = jnp.where(lane == 1, i2.astype(F32), info)
    info = jnp.where(lane == 2, g1, info)
    info = jnp.where(lane == 3, g2, info)
    info = jnp.where(lane == 4, rank1, info)
    info = jnp.where(lane == 5, rank2, info)
    info_ref[...] = info[:, :INFO_COLS]


def _route(o, w, res, gain, wr_hi, wr_lo):
    t, d = res.shape
    k = o.shape[1]
    tm = TOKEN_TILE
    lt = jnp.tril(jnp.ones((tm, tm), BF16), -1)
    est = 2 * (tm * k * 2 + k * d * 2 + 3 * tm * d * 4 + 2 * d * LANES * 2 + tm * tm * 2 + tm * LANES * 4) + 6 * tm * d * 4
    row = lambda i: (i, 0)
    return pl.pallas_call(
        _route_kernel,
        out_shape=(jax.ShapeDtypeStruct((t, d), F32), jax.ShapeDtypeStruct((t, d), F32),
                   jax.ShapeDtypeStruct((t, INFO_COLS), F32), jax.ShapeDtypeStruct((8, LANES), F32)),
        grid=(t // tm,),
        in_specs=[pl.BlockSpec((tm, k), row), _resident((k, d)), pl.BlockSpec((tm, d), row), _resident((1, d)),
                  _resident((d, LANES)), _resident((d, LANES)), _resident((tm, tm))],
        out_specs=(pl.BlockSpec((tm, d), row), pl.BlockSpec((tm, d), row), pl.BlockSpec((tm, INFO_COLS), row),
                   _resident((8, LANES))),
        scratch_shapes=[pltpu.VMEM((8, LANES), F32)],
        compiler_params=_params(("arbitrary",), est),
        name="outproj_norm_router",
    )(o, w, res, gain, wr_hi, wr_lo, lt)


def _scatter_kernel(gap_lo_ref, gap_hi_ref, pos1_ref, pos2_ref, src_ref, dst_ref, zero_ref, sem, zero_sem):
    tile = pos1_ref.shape[0]

    @pl.when(pl.program_id(0) == 0)
    def _():
        zero_ref[...] = jnp.zeros_like(zero_ref)

        sub = zero_ref.shape[0]

        def row_copy(r):
            return pltpu.make_async_copy(zero_ref.at[pl.ds(0, 1)], dst_ref.at[pl.ds(r, 1)], zero_sem)

        def tile_copy(k):
            return pltpu.make_async_copy(zero_ref, dst_ref.at[pl.ds(pl.multiple_of(k * sub, sub), sub)], zero_sem)

        def run(make, lo, hi, wait):
            def body(r, c):
                cp = make(r)
                cp.wait() if wait else cp.start()
                return c
            lax.fori_loop(lo, hi, body, 0)

        for wait in (False, True):
            for g in range(gap_lo_ref.shape[0]):
                lo, hi = gap_lo_ref[g], gap_hi_ref[g]
                mid = jnp.minimum((lo + sub - 1) // sub * sub, hi)
                run(row_copy, lo, mid, wait)
                run(tile_copy, mid // sub, hi // sub, wait)

    def copies(r):
        src = src_ref.at[pl.ds(r, 1)]
        return (pltpu.make_async_copy(src, dst_ref.at[pl.ds(pos1_ref[r], 1)], sem),
                pltpu.make_async_copy(src, dst_ref.at[pl.ds(pos2_ref[r], 1)], sem))

    def start(r, c):
        for cp in copies(r):
            cp.start()
        return c

    def wait(r, c):
        for cp in copies(r):
            cp.wait()
        return c

    lax.fori_loop(0, tile, start, 0, unroll=ROW_UNROLL)
    lax.fori_loop(0, tile, wait, 0, unroll=ROW_UNROLL)


def _scatter_rows(gap_lo, gap_hi, pos1, pos2, src, n_sorted):
    t, d = src.shape
    tile = ROW_TILE
    smem = lambda: pl.BlockSpec((tile,), lambda i, lo, hi: (i,), memory_space=pltpu.SMEM)
    grid_spec = pltpu.PrefetchScalarGridSpec(
        num_scalar_prefetch=2,
        grid=(t // tile,),
        in_specs=[smem(), smem(), pl.BlockSpec((tile, d), lambda i, lo, hi: (i, 0))],
        out_specs=pl.BlockSpec(memory_space=pl.ANY),
        scratch_shapes=[pltpu.VMEM((8, d), src.dtype), pltpu.SemaphoreType.DMA(()), pltpu.SemaphoreType.DMA(())])
    return pl.pallas_call(
        _scatter_kernel,
        out_shape=jax.ShapeDtypeStruct((n_sorted, d), src.dtype),
        grid_spec=grid_spec,
        compiler_params=pltpu.CompilerParams(dimension_semantics=("arbitrary",),
                                             vmem_limit_bytes=_vmem_limit(2 * tile * d * 4)),
        name="moe_row_scatter",
    )(gap_lo, gap_hi, pos1, pos2, src)


def _moe_kernel(te_ref, nv_ref, x_ref, wg_ref, wu_ref, wd_ref, y_ref, xb_ref, act_ref, acc_ref, *, f_chunk):
    del te_ref
    j = pl.program_id(0)
    f = pl.program_id(1)
    nf = pl.num_programs(1)
    tf = wg_ref.shape[1]

    @pl.when(j < nv_ref[0])
    def _():
        @pl.when(f == 0)
        def _():
            xb_ref[...] = x_ref[...].astype(BF16)

        xb = xb_ref[...]
        for c0 in range(0, tf, f_chunk):
            cols = slice(c0, c0 + f_chunk)
            act_ref[:, cols] = (_silu(_dot(xb, wg_ref[:, cols])) * _dot(xb, wu_ref[:, cols])).astype(BF16)
        part = _dot(act_ref[...], wd_ref[...])

        @pl.when(jnp.logical_and(f == 0, nf > 1))
        def _():
            acc_ref[...] = part

        @pl.when(jnp.logical_and(f > 0, f < nf - 1))
        def _():
            acc_ref[...] += part

        @pl.when(jnp.logical_and(f == nf - 1, nf > 1))
        def _():
            y_ref[...] = acc_ref[...] + part

        @pl.when(nf == 1)
        def _():
            y_ref[...] = part

    @pl.when(jnp.logical_and(j >= nv_ref[0], f == nf - 1))
    def _():
        y_ref[...] = jnp.zeros_like(y_ref)


def _moe_f_tile(f_dim):
    nf = 1
    while f_dim % nf or f_dim // nf > MOE_F_TILE or (f_dim // nf) % MOE_F_CHUNK:
        nf += 1
    return f_dim // nf


def _moe(tile_expert, n_valid, xs, wg, wu, wd):
    p, d = xs.shape
    f_dim = wg.shape[2]
    tm, tf = MOE_TILE, _moe_f_tile(f_dim)
    nf = f_dim // tf
    w_bufs = 1 if nf == 1 else 2
    w_mode = pl.Buffered(1) if nf == 1 else None

    def tile_idx(j, nv):
        return jnp.minimum(j, nv[0] - 1)

    def f_idx(j, f, nv):
        return jnp.where(j < nv[0], f, nf - 1)

    grid_spec = pltpu.PrefetchScalarGridSpec(
        num_scalar_prefetch=2,
        grid=(p // tm, nf),
        in_specs=[pl.BlockSpec((tm, d), lambda j, f, te, nv: (tile_idx(j, nv), 0)),
                  pl.BlockSpec((None, d, tf), lambda j, f, te, nv: (te[tile_idx(j, nv)], 0, f_idx(j, f, nv)),
                               pipeline_mode=w_mode),
                  pl.BlockSpec((None, d, tf), lambda j, f, te, nv: (te[tile_idx(j, nv)], 0, f_idx(j, f, nv)),
                               pipeline_mode=w_mode),
                  pl.BlockSpec((None, tf, d), lambda j, f, te, nv: (te[tile_idx(j, nv)], f_idx(j, f, nv), 0),
                               pipeline_mode=w_mode)],
        out_specs=pl.BlockSpec((tm, d), lambda j, f, te, nv: (j, 0)),
        scratch_shapes=[pltpu.VMEM((tm, d), BF16), pltpu.VMEM((tm, tf), BF16), pltpu.VMEM((tm, d), F32)])
    est = (2 * 2 * tm * d * 4 + w_bufs * 3 * d * tf * 2 + tm * d * 6 + tm * tf * 2 + 6 * tm * MOE_F_CHUNK * 4
           + tm * d * 4)
    return pl.pallas_call(
        functools.partial(_moe_kernel, f_chunk=MOE_F_CHUNK),
        out_shape=jax.ShapeDtypeStruct((p, d), F32),
        grid_spec=grid_spec,
        compiler_params=_params(("arbitrary", "arbitrary"), est),
        name="moe_grouped_swiglu",
    )(tile_expert, n_valid, xs, wg, wu, wd)


def _combine_kernel(pos1_ref, pos2_ref, nxt1_ref, nxt2_ref, h_ref, info_ref, g_ref, y_ref, out_ref, buf_ref, sem):
    tile = h_ref.shape[0]
    i = pl.program_id(0)
    slot = lax.rem(i, 2)

    def copies(p1_ref, p2_ref, s, r):
        return (pltpu.make_async_copy(y_ref.at[pl.ds(p1_ref[r], 1)], buf_ref.at[s, 0, pl.ds(r, 1)], sem.at[s]),
                pltpu.make_async_copy(y_ref.at[pl.ds(p2_ref[r], 1)], buf_ref.at[s, 1, pl.ds(r, 1)], sem.at[s]))

    def start_all(p1_ref, p2_ref, s):
        def start(r, c):
            for cp in copies(p1_ref, p2_ref, s, r):
                cp.start()
            return c
        lax.fori_loop(0, tile, start, 0, unroll=ROW_UNROLL)

    @pl.when(i == 0)
    def _():
        start_all(pos1_ref, pos2_ref, slot)

    @pl.when(i + 1 < pl.num_programs(0))
    def _():
        start_all(nxt1_ref, nxt2_ref, 1 - slot)

    def wait(r, c):
        for cp in copies(pos1_ref, pos2_ref, slot, r):
            cp.wait()
        return c

    lax.fori_loop(0, tile, wait, 0, unroll=ROW_UNROLL)
    info = info_ref[...]
    h = h_ref[...] + (info[:, 2:3] * buf_ref[slot, 0] + info[:, 3:4] * buf_ref[slot, 1])
    out_ref[...] = h * _rms_scale(h) * g_ref[...]


def _combine(pos1, pos2, h, info, gain, y):
    t, d = h.shape
    tile = ROW_TILE
    n = t // tile
    smem = lambda step: pl.BlockSpec((tile,), lambda i: (jnp.minimum(i + step, n - 1),), memory_space=pltpu.SMEM)
    row = lambda i: (i, 0)
    est = 2 * (2 * tile * d * 4 + tile * LANES * 4) + 4 * tile * d * 4 + 4 * tile * d * 4
    return pl.pallas_call(
        _combine_kernel,
        out_shape=jax.ShapeDtypeStruct((t, d), F32),
        grid=(n,),
        in_specs=[smem(0), smem(0), smem(1), smem(1), pl.BlockSpec((tile, d), row), pl.BlockSpec((tile, INFO_COLS), row),
                  _resident((1, d)), pl.BlockSpec(memory_space=pl.ANY)],
        out_specs=pl.BlockSpec((tile, d), row),
        scratch_shapes=[pltpu.VMEM((2, 2, tile, d), F32), pltpu.SemaphoreType.DMA((2,))],
        compiler_params=_params(("arbitrary",), est),
        name="moe_combine_final_norm",
    )(pos1, pos2, pos1, pos2, h, info, gain, y)


def kernel(x, attn_norm, ffn_norm, kv_norm, final_norm, gla_w_in, gla_w_gate_up, gla_b_gate, gla_head_norm,
           gla_w_out, sb_w_kv, sb_w_q, sb_w_out, ffn_w_gate, ffn_w_up, ffn_w_down, moe_w_router, moe_w_gate,
           moe_w_up, moe_w_down):
    b_sz, s_len, d = x.shape
    t = b_sz * s_len
    heads = GLA_HEADS
    d_qk = gla_w_gate_up.shape[2] * 2
    hk = d_qk // 2
    rank = gla_w_gate_up.shape[1]
    d_v = gla_w_out.shape[1]
    dk, dv = hk // heads, d_v // heads
    assert gla_w_in.shape[2] == d_qk + 2 * d_v + rank and rank <= LANES
    assert t % TOKEN_TILE == 0 and s_len % GLA_TILE == 0 and s_len % SB_Q_TILE == 0
    assert attn_norm.shape[0] == 2 and moe_w_gate.shape[1] == N_EXPERTS

    xf = x.reshape(t, d)
    gain = lambda g: g.reshape(1, -1).astype(F32)

    w_in = gla_w_in[0]
    c_v, c_g, c_r = d_qk, d_qk + d_v, d_qk + d_v + rank
    w_in_cat = jnp.concatenate(
        [w_in[:, :c_v], w_in[:, c_v:c_g], w_in[:, c_r:], jnp.pad(w_in[:, c_g:c_r], ((0, 0), (0, LANES - rank)))],
        axis=1).astype(BF16)
    qk, v, r, glr = _inproj(xf, gain(attn_norm[0]), w_in_cat, d_qk, d_v)
    wgu = jnp.pad(gla_w_gate_up[0], ((0, LANES - rank), (0, 0))).astype(BF16)
    h, hn = _gla(qk, v, r, glr, wgu, gla_b_gate[0].reshape(1, hk), gain(gla_head_norm[0]),
                 gla_w_out[0].astype(BF16), xf, gain(ffn_norm[0]), b_sz, s_len, heads, dk, dv)

    h = _ffn(hn, ffn_w_gate[0].astype(BF16), ffn_w_up[0].astype(BF16), ffn_w_down[0].astype(BF16), h)

    hd = sb_w_q.shape[2] // SB_HEADS
    kv, q = _kvq(h, gain(kv_norm), gain(attn_norm[1]), sb_w_kv.astype(BF16), sb_w_q[0].astype(BF16),
                 hd ** -0.5 * LOG2_E)
    o = _sb_attention(q, kv, b_sz, s_len, SB_HEADS, hd)

    wr = jnp.pad(moe_w_router[0], ((0, 0), (0, LANES - N_EXPERTS)))
    wr_hi = wr.astype(BF16)
    wr_lo = (wr - wr_hi.astype(F32)).astype(BF16)
    h, hn, info, counts = _route(o, sb_w_out[0].astype(BF16), h, gain(ffn_norm[1]), wr_hi, wr_lo)

    cnt = counts[0, :N_EXPERTS].astype(jnp.int32)
    padded = (cnt + MOE_TILE - 1) // MOE_TILE * MOE_TILE
    ends = jnp.cumsum(padded)
    offs = ends - padded
    e1, e2 = info[:, 0].astype(jnp.int32), info[:, 1].astype(jnp.int32)
    pos1 = offs[e1] + info[:, 4].astype(jnp.int32)
    pos2 = offs[e2] + info[:, 5].astype(jnp.int32)
    n_sorted = 2 * t + N_EXPERTS * MOE_TILE
    n_tiles = n_sorted // MOE_TILE
    tile_start = jnp.arange(n_tiles, dtype=jnp.int32) * MOE_TILE
    tile_expert = jnp.minimum(jnp.sum(tile_start[:, None] >= ends[None, :], axis=1), N_EXPERTS - 1).astype(jnp.int32)
    n_valid = (ends[-1] // MOE_TILE).astype(jnp.int32).reshape(1)

    gap_lo = jnp.concatenate([offs + cnt, ends[-1:]]).astype(jnp.int32)
    gap_hi = jnp.concatenate([ends, jnp.full((1,), n_sorted, ends.dtype)]).astype(jnp.int32)
    xs = _scatter_rows(gap_lo, gap_hi, pos1, pos2, hn, n_sorted)
    y = _moe(tile_expert, n_valid, xs, moe_w_gate[0].astype(BF16), moe_w_up[0].astype(BF16),
             moe_w_down[0].astype(BF16))
    out = _combine(pos1, pos2, h, info, gain(final_norm), y)
    return out.reshape(b_sz, s_len, d)
```
